```python
import jax, jax.numpy as jnp
from jax import lax
import numpy as np

D_MODEL = 4096
BATCH = 4
SEQ = 2048
DEPTH = 1

CONV_DIM = 2048
CONV_WIDTH = 3
N_HEADS = 16
HEAD_DIM = 128
ATTN_DIM = N_HEADS * HEAD_DIM
MOBA_BLOCK = 256
MOBA_TOPK = 3
Q_CHUNK = 16
N_BRANCH = 2
SPLIT_SIZES = [CONV_DIM] * 3 + [ATTN_DIM] * 3 + [D_MODEL] * N_BRANCH
SPLIT_IDX = [int(i) for i in np.cumsum(SPLIT_SIZES)[:-1]]
IN_COLS = int(sum(SPLIT_SIZES))
PEER_HEADS = 8
N_KEYS = 128
N_EXPERTS = N_KEYS * N_KEYS
PEER_KEY_DIM = 256
PEER_HALF = PEER_KEY_DIM // 2
PEER_TOPK = 16
TOKEN_CHUNK = 128
EPS = 1e-6

kernel_name = "hybrid_conv_moba_peer_block"


def rms_norm(x, g):
    xf = x.astype(jnp.float32)
    y = xf * lax.rsqrt(jnp.mean(xf * xf, axis=-1, keepdims=True) + EPS)
    return (y * g.astype(jnp.float32)).astype(x.dtype)


def short_conv_mixer(b_gate, c_gate, u, conv_w, w_out):
    z = c_gate * u
    z = lax.conv_general_dilated(
        z, conv_w[:, None, :].astype(z.dtype), window_strides=(1,),
        padding=[(CONV_WIDTH - 1, 0)], dimension_numbers=("NWC", "WIO", "NWC"),
        feature_group_count=CONV_DIM)
    return (b_gate * z) @ w_out


def moba_attention(q, k, v, w_out):
    b_, s_, h_, hd = q.shape
    f32 = jnp.float32
    n_blk = -(-s_ // MOBA_BLOCK)
    sp = n_blk * MOBA_BLOCK
    pad = [(0, 0), (0, sp - s_), (0, 0), (0, 0)]
    q, k, v = [jnp.pad(t, pad).transpose(0, 2, 1, 3) for t in (q, k, v)]
    k_blk = k.reshape(b_, h_, n_blk, MOBA_BLOCK, hd)
    v_blk = v.reshape(b_, h_, n_blk, MOBA_BLOCK, hd)
    k_mean = jnp.mean(k_blk.astype(f32), axis=3)
    q_blk_id = jnp.arange(sp) // MOBA_BLOCK
    gate = jnp.einsum("bhsd,bhnd->bhsn", q.astype(f32), k_mean)
    past = jnp.arange(n_blk)[None, :] < q_blk_id[:, None]
    gate = jnp.where(past, gate, -jnp.inf)
    n_sel = min(MOBA_TOPK, n_blk)
    _, sel = lax.top_k(gate, n_sel)
    sel_valid = sel < q_blk_id[:, None]
    n_chunk = sp // Q_CHUNK
    scale = HEAD_DIM ** -0.5
    bi = jnp.arange(b_)[:, None, None, None]
    hi = jnp.arange(h_)[None, :, None, None]

    def chunk(args):
        qc, selc, validc, c = args
        start = c * Q_CHUNK
        own = start // MOBA_BLOCK
        k_own = lax.dynamic_index_in_dim(k_blk, own, axis=2, keepdims=False)
        v_own = lax.dynamic_index_in_dim(v_blk, own, axis=2, keepdims=False)
        q_pos = start + jnp.arange(Q_CHUNK)
        k_pos = own * MOBA_BLOCK + jnp.arange(MOBA_BLOCK)
        s_own = jnp.einsum("bhqd,bhkd->bhqk", qc, k_own).astype(f32) * scale
        s_own = jnp.where(k_pos[None, :] <= q_pos[:, None], s_own, -jnp.inf)
        k_sel = k_blk[bi, hi, selc]
        v_sel = v_blk[bi, hi, selc]
        s_sel = jnp.einsum("bhqd,bhqnkd->bhqnk", qc, k_sel).astype(f32) * scale
        s_sel = jnp.where(validc[..., None], s_sel, -jnp.inf)
        logits = jnp.concatenate([s_own, s_sel.reshape(b_, h_, Q_CHUNK, -1)], axis=-1)
        p = jax.nn.softmax(logits, axis=-1).astype(qc.dtype)
        p_own = p[..., :MOBA_BLOCK]
        p_sel = p[..., MOBA_BLOCK:].reshape(b_, h_, Q_CHUNK, n_sel, MOBA_BLOCK)
        return (jnp.einsum("bhqk,bhkd->bhqd", p_own, v_own)
                + jnp.einsum("bhqnk,bhqnkd->bhqd", p_sel, v_sel))

    qs = q.reshape(b_, h_, n_chunk, Q_CHUNK, hd).transpose(2, 0, 1, 3, 4)
    sels = sel.reshape(b_, h_, n_chunk, Q_CHUNK, n_sel).transpose(2, 0, 1, 3, 4)
    vals = sel_valid.reshape(b_, h_, n_chunk, Q_CHUNK, n_sel).transpose(2, 0, 1, 3, 4)
    out = lax.map(chunk, (qs, sels, vals, jnp.arange(n_chunk)))
    out = out.transpose(1, 0, 3, 2, 4).reshape(b_, sp, h_ * hd)[:, :s_]
    return out @ w_out


def peer_ffn(x, w_q, sub_keys, u_emb, v_emb):
    b_, s_, d_ = x.shape
    xt = x.reshape(-1, d_)
    t_ = xt.shape[0]
    q = (xt @ w_q).reshape(t_, PEER_HEADS, 2, PEER_HALF)
    s = jnp.einsum("thcd,hckd->thck", q, sub_keys).astype(jnp.float32)
    top_s, top_i = lax.top_k(s, PEER_TOPK)
    cand_s = top_s[:, :, 0, :, None] + top_s[:, :, 1, None, :]
    cand_i = top_i[:, :, 0, :, None] * N_KEYS + top_i[:, :, 1, None, :]
    best_s, best_j = lax.top_k(cand_s.reshape(t_, PEER_HEADS, -1), PEER_TOPK)
    expert = jnp.take_along_axis(cand_i.reshape(t_, PEER_HEADS, -1), best_j, axis=-1)
    g = jax.nn.softmax(best_s, axis=-1).astype(x.dtype)
    n_chunk = t_ // TOKEN_CHUNK

    def chunk(args):
        xc, ec, gc = args
        hid = jax.nn.gelu(jnp.einsum("td,thkd->thk", xc, u_emb[ec]), approximate=False) * gc
        return jnp.einsum("thk,thkd->td", hid, v_emb[ec])

    out = lax.map(chunk, (xt.reshape(n_chunk, TOKEN_CHUNK, d_),
                          expert.reshape(n_chunk, TOKEN_CHUNK, PEER_HEADS, PEER_TOPK),
                          g.reshape(n_chunk, TOKEN_CHUNK, PEER_HEADS, PEER_TOPK)))
    return out.reshape(b_, s_, d_)


def setup_inputs(seed: int = 0) -> dict:
    key = jax.random.key(seed)
    ks = jax.random.split(key, 16)
    nrm = jax.random.normal
    f32 = jnp.float32
    return {
        "x": nrm(ks[0], (BATCH, SEQ, D_MODEL), f32),
        "norm_mix": 1.0 + 0.02 * nrm(ks[1], (DEPTH, D_MODEL), f32),
        "w_in": nrm(ks[2], (DEPTH, D_MODEL, IN_COLS), f32) * D_MODEL ** -0.5,
        "b_gate": 0.02 * nrm(ks[3], (DEPTH, N_BRANCH * D_MODEL), f32),
        "conv_w": nrm(ks[4], (DEPTH, CONV_WIDTH, CONV_DIM), f32) * CONV_WIDTH ** -0.5,
        "w_conv_out": nrm(ks[5], (DEPTH, CONV_DIM, D_MODEL), f32) * CONV_DIM ** -0.5,
        "w_attn_out": nrm(ks[6], (DEPTH, ATTN_DIM, D_MODEL), f32) * ATTN_DIM ** -0.5,
        "w_o": nrm(ks[7], (DEPTH, D_MODEL, D_MODEL), f32) * D_MODEL ** -0.5,
        "norm_ffn": 1.0 + 0.02 * nrm(ks[8], (DEPTH, D_MODEL), f32),
        "w_peer_q": nrm(ks[9], (DEPTH, D_MODEL, PEER_HEADS * PEER_KEY_DIM), f32) * D_MODEL ** -0.5,
        "sub_keys": nrm(ks[10], (DEPTH, PEER_HEADS, 2, N_KEYS, PEER_HALF), f32) * PEER_HALF ** -0.5,
        "u_emb": nrm(ks[11], (DEPTH, N_EXPERTS, D_MODEL), f32) * D_MODEL ** -0.5,
        "v_emb": nrm(ks[12], (DEPTH, N_EXPERTS, D_MODEL), f32) * PEER_HEADS ** -0.5,
        "norm_final": 1.0 + 0.02 * nrm(ks[13], (D_MODEL,), f32),
    }


def reference(x, norm_mix, w_in, b_gate, conv_w, w_conv_out, w_attn_out, w_o,
              norm_ffn, w_peer_q, sub_keys, u_emb, v_emb, norm_final):
    h = x
    b_, s_, _ = x.shape
    for l in range(DEPTH):
        hn = rms_norm(h, norm_mix[l])
        z = hn @ w_in[l]
        bc, cc, uc, q, k, v, gc, ga = jnp.split(z, SPLIT_IDX, axis=-1)
        bias_c, bias_a = jnp.split(b_gate[l], 2)
        y_conv = short_conv_mixer(bc, cc, uc, conv_w[l], w_conv_out[l])
        y_attn = moba_attention(q.reshape(b_, s_, N_HEADS, HEAD_DIM),
                                k.reshape(b_, s_, N_HEADS, HEAD_DIM),
                                v.reshape(b_, s_, N_HEADS, HEAD_DIM), w_attn_out[l])
        merged = jax.nn.sigmoid(gc + bias_c) * y_conv + jax.nn.sigmoid(ga + bias_a) * y_attn
        h = h + merged @ w_o[l]
        h = h + peer_ffn(rms_norm(h, norm_ffn[l]), w_peer_q[l], sub_keys[l], u_emb[l], v_emb[l])
    return rms_norm(h, norm_final)
```

```python
import functools

import jax
import jax.numpy as jnp
from jax import lax
from jax.experimental import pallas as pl
from jax.experimental.pallas import tpu as pltpu

D_MODEL = 4096
CONV_DIM = 2048
CONV_WIDTH = 3
N_HEADS = 16
HEAD_DIM = 128
ATTN_DIM = N_HEADS * HEAD_DIM
MOBA_BLOCK = 256
MOBA_TOPK = 3
PEER_HEADS = 8
N_KEYS = 128
N_EXPERTS = N_KEYS * N_KEYS
PEER_HALF = 128
PEER_TOPK = 16
EPS = 1e-6

COL_B = 0
COL_C = CONV_DIM
COL_U = 2 * CONV_DIM
COL_Q = 3 * CONV_DIM
COL_K = COL_Q + ATTN_DIM
COL_V = COL_K + ATTN_DIM
COL_GC = COL_V + ATTN_DIM
COL_GA = COL_GC + D_MODEL
IN_COLS = COL_GA + D_MODEL

VMEM_LIMIT_BYTES = 56 * 1024 * 1024

F32 = jnp.float32
BF16 = jnp.bfloat16
NEG_INF = float("-inf")


def _params(*semantics):
    return pltpu.CompilerParams(dimension_semantics=semantics,
                                vmem_limit_bytes=VMEM_LIMIT_BYTES)


def _rmsnorm_kernel(x_ref, g_ref, o_ref):
    x = x_ref[...]
    y = x * lax.rsqrt(jnp.mean(x * x, axis=-1, keepdims=True) + EPS)
    o_ref[...] = (y * g_ref[...]).astype(o_ref.dtype)


def _rmsnorm(x, g, out_dtype, tm=256):
    m, d = x.shape
    return pl.pallas_call(
        _rmsnorm_kernel,
        grid=(m // tm,),
        in_specs=[pl.BlockSpec((tm, d), lambda i: (i, 0)),
                  pl.BlockSpec((1, d), lambda i: (0, 0))],
        out_specs=pl.BlockSpec((tm, d), lambda i: (i, 0)),
        out_shape=jax.ShapeDtypeStruct((m, d), out_dtype),
        compiler_params=_params("parallel"),
        name="rmsnorm",
    )(x, g.reshape(1, d))


def _add_rmsnorm_kernel(a_ref, b_ref, g_ref, o_ref):
    x = a_ref[...] + b_ref[...]
    y = x * lax.rsqrt(jnp.mean(x * x, axis=-1, keepdims=True) + EPS)
    o_ref[...] = (y * g_ref[...]).astype(o_ref.dtype)


def _add_rmsnorm(a, b, g, tm=256):
    m, d = a.shape
    return pl.pallas_call(
        _add_rmsnorm_kernel,
        grid=(m // tm,),
        in_specs=[pl.BlockSpec((tm, d), lambda i: (i, 0)),
                  pl.BlockSpec((tm, d), lambda i: (i, 0)),
                  pl.BlockSpec((1, d), lambda i: (0, 0))],
        out_specs=pl.BlockSpec((tm, d), lambda i: (i, 0)),
        out_shape=jax.ShapeDtypeStruct((m, d), F32),
        compiler_params=_params("parallel"),
        name="add_rmsnorm",
    )(a, b, g.reshape(1, d))


def _matmul_kernel(a_ref, w_ref, o_ref):
    o_ref[...] = jnp.dot(a_ref[...], w_ref[...],
                         preferred_element_type=F32).astype(o_ref.dtype)


def _matmul(a, w, out_dtype, tm=1024, tn=1024):
    m, k = a.shape
    _, n = w.shape
    return pl.pallas_call(
        _matmul_kernel,
        grid=(n // tn, m // tm),
        in_specs=[pl.BlockSpec((tm, k), lambda j, i: (i, 0)),
                  pl.BlockSpec((k, tn), lambda j, i: (0, j))],
        out_specs=pl.BlockSpec((tm, tn), lambda j, i: (i, j)),
        out_shape=jax.ShapeDtypeStruct((m, n), out_dtype),
        compiler_params=_params("parallel", "parallel"),
        name="matmul",
    )(a, w)


def _matmul_residual_kernel(a_ref, w_ref, r_ref, o_ref):
    o_ref[...] = r_ref[...] + jnp.dot(a_ref[...], w_ref[...], preferred_element_type=F32)


def _matmul_residual(a, w, r, tm=1024, tn=1024):
    m, k = a.shape
    _, n = w.shape
    return pl.pallas_call(
        _matmul_residual_kernel,
        grid=(n // tn, m // tm),
        in_specs=[pl.BlockSpec((tm, k), lambda j, i: (i, 0)),
                  pl.BlockSpec((k, tn), lambda j, i: (0, j)),
                  pl.BlockSpec((tm, tn), lambda j, i: (i, j))],
        out_specs=pl.BlockSpec((tm, tn), lambda j, i: (i, j)),
        out_shape=jax.ShapeDtypeStruct((m, n), F32),
        compiler_params=_params("parallel", "parallel"),
        name="matmul_residual",
    )(a, w, r)


def _conv_kernel(b_ref, c_ref, u_ref, w_ref, o_ref):
    z = c_ref[...] * u_ref[...]
    row = lax.broadcasted_iota(jnp.int32, z.shape, 0)
    w = w_ref[...]
    acc = z * w[2:3, :]
    for shift in (1, 2):
        zs = jnp.where(row >= shift, pltpu.roll(z, shift, axis=0), 0.0)
        acc = acc + zs * w[2 - shift:3 - shift, :]
    o_ref[...] = (b_ref[...] * acc).astype(o_ref.dtype)


def _short_conv(z, conv_w, batch, seq, tn=256):
    nb = CONV_DIM // tn
    return pl.pallas_call(
        _conv_kernel,
        grid=(batch, nb),
        in_specs=[pl.BlockSpec((seq, tn), lambda b, j: (b, COL_B // tn + j)),
                  pl.BlockSpec((seq, tn), lambda b, j: (b, COL_C // tn + j)),
                  pl.BlockSpec((seq, tn), lambda b, j: (b, COL_U // tn + j)),
                  pl.BlockSpec((CONV_WIDTH, tn), lambda b, j: (0, j))],
        out_specs=pl.BlockSpec((seq, tn), lambda b, j: (b, j)),
        out_shape=jax.ShapeDtypeStruct((batch * seq, CONV_DIM), BF16),
        compiler_params=_params("parallel", "parallel"),
        name="short_conv",
    )(z, z, z, conv_w)


def _moba_kernel(q_ref, k_ref, v_ref, o_ref, *, n_blk):
    scale = HEAD_DIM ** -0.5
    k = k_ref[...]
    kb = k.astype(BF16)
    vb = v_ref[...].astype(BF16)
    k_mean = jnp.mean(k.reshape(n_blk, MOBA_BLOCK, HEAD_DIM), axis=1)
    lane = lax.broadcasted_iota(jnp.int32, (MOBA_BLOCK, n_blk), 1)
    row = lax.broadcasted_iota(jnp.int32, (MOBA_BLOCK, MOBA_BLOCK), 0)
    col = lax.broadcasted_iota(jnp.int32, (MOBA_BLOCK, MOBA_BLOCK), 1)
    causal = col <= row
    for i in range(n_blk):
        q = q_ref[i * MOBA_BLOCK:(i + 1) * MOBA_BLOCK, :]
        nk = (i + 1) * MOBA_BLOCK
        s = lax.dot_general(q.astype(BF16), kb[:nk], (((1,), (1,)), ((), ())),
                            preferred_element_type=F32) * scale
        parts = []
        if i > 0:
            gate = lax.dot_general(q, k_mean, (((1,), (1,)), ((), ())),
                                   precision=lax.Precision.HIGHEST,
                                   preferred_element_type=F32)
            gate = jnp.where(lane < i, gate, NEG_INF)
            rank = jnp.zeros((MOBA_BLOCK, n_blk), jnp.int32)
            for jp in range(i):
                gj = gate[:, jp:jp + 1]
                ahead = (gj > gate) | ((gj == gate) & (lane > jp))
                rank = rank + ahead.astype(jnp.int32)
            sel = (rank < MOBA_TOPK) & (lane < i)
            bias = jnp.where(sel, 0.0, NEG_INF)
            for j in range(i):
                parts.append(s[:, j * MOBA_BLOCK:(j + 1) * MOBA_BLOCK] + bias[:, j:j + 1])
        parts.append(jnp.where(causal, s[:, i * MOBA_BLOCK:nk], NEG_INF))
        s = parts[0] if len(parts) == 1 else jnp.concatenate(parts, axis=1)
        m = jnp.max(s, axis=-1, keepdims=True)
        p = jnp.exp(s - m)
        l = jnp.sum(p, axis=-1, keepdims=True)
        out = jnp.dot(p.astype(BF16), vb[:nk], preferred_element_type=F32) / l
        o_ref[i * MOBA_BLOCK:(i + 1) * MOBA_BLOCK, :] = out.astype(o_ref.dtype)


def _moba_attention(z, batch, seq):
    n_blk = seq // MOBA_BLOCK
    spec = lambda off: pl.BlockSpec((seq, HEAD_DIM), lambda b, h: (b, off // HEAD_DIM + h))
    return pl.pallas_call(
        functools.partial(_moba_kernel, n_blk=n_blk),
        grid=(batch, N_HEADS),
        in_specs=[spec(COL_Q), spec(COL_K), spec(COL_V)],
        out_specs=pl.BlockSpec((seq, HEAD_DIM), lambda b, h: (b, h)),
        out_shape=jax.ShapeDtypeStruct((batch * seq, ATTN_DIM), BF16),
        compiler_params=_params("parallel", "parallel"),
        name="moba_attention",
    )(z, z, z)


def _merge_kernel(bz_ref, at_ref, wc_ref, wa_ref, gc_ref, ga_ref, bc_ref, ba_ref, o_ref):
    y_conv = jnp.dot(bz_ref[...], wc_ref[...], preferred_element_type=F32)
    y_attn = jnp.dot(at_ref[...], wa_ref[...], preferred_element_type=F32)
    merged = (jax.nn.sigmoid(gc_ref[...] + bc_ref[...]) * y_conv
              + jax.nn.sigmoid(ga_ref[...] + ba_ref[...]) * y_attn)
    o_ref[...] = merged.astype(o_ref.dtype)


def _merge(bz, attn, w_conv_out, w_attn_out, z, b_gate, tm=512, tn=1024):
    m = bz.shape[0]
    nb = D_MODEL // tn
    bias = b_gate.reshape(1, 2 * D_MODEL)
    return pl.pallas_call(
        _merge_kernel,
        grid=(nb, m // tm),
        in_specs=[pl.BlockSpec((tm, CONV_DIM), lambda j, i: (i, 0)),
                  pl.BlockSpec((tm, ATTN_DIM), lambda j, i: (i, 0)),
                  pl.BlockSpec((CONV_DIM, tn), lambda j, i: (0, j)),
                  pl.BlockSpec((ATTN_DIM, tn), lambda j, i: (0, j)),
                  pl.BlockSpec((tm, tn), lambda j, i: (i, COL_GC // tn + j)),
                  pl.BlockSpec((tm, tn), lambda j, i: (i, COL_GA // tn + j)),
                  pl.BlockSpec((1, tn), lambda j, i: (0, j)),
                  pl.BlockSpec((1, tn), lambda j, i: (0, nb + j))],
        out_specs=pl.BlockSpec((tm, tn), lambda j, i: (i, j)),
        out_shape=jax.ShapeDtypeStruct((m, D_MODEL), BF16),
        compiler_params=_params("parallel", "parallel"),
        name="merge",
    )(bz, attn, w_conv_out, w_attn_out, z, z, bias, bias)


def _extract_top(s, n):
    rows = s.shape[0]
    idx = lax.broadcasted_iota(jnp.int32, s.shape, 0)
    vals = []
    for _ in range(n):
        m = jnp.max(s, axis=0, keepdims=True)
        first = jnp.min(jnp.where(s == m, idx, rows), axis=0, keepdims=True)
        s = jnp.where(idx == first, NEG_INF, s)
        vals.append(m)
    return vals


def _route_kernel(q_ref, sk_ref, s1_ref, s2_ref, e1_ref, e2_ref, tau_ref):
    q = q_ref[...]
    dims = (((1,), (1,)), ((), ()))
    s1 = lax.dot_general(sk_ref[0, 0], q[:, :PEER_HALF], dims,
                         precision=lax.Precision.HIGHEST, preferred_element_type=F32)
    s2 = lax.dot_general(sk_ref[0, 1], q[:, PEER_HALF:], dims,
                         precision=lax.Precision.HIGHEST, preferred_element_type=F32)
    t1 = _extract_top(s1, PEER_TOPK)
    t2 = jnp.concatenate(_extract_top(s2, PEER_TOPK), axis=0)
    cand = jnp.concatenate([t1[a] + t2 for a in range(PEER_TOPK)], axis=0)
    best = _extract_top(cand, PEER_TOPK)
    z = jnp.ones_like(best[0])
    for r in range(1, PEER_TOPK):
        z = z + jnp.exp(best[r] - best[0])
    s1_ref[0] = s1
    s2_ref[0] = s2
    e1_ref[0] = jnp.exp(s1 - t1[0]) / z
    e2_ref[0] = jnp.exp(s2 - t2[0:1, :])
    tau_ref[0] = best[PEER_TOPK - 1]


def _peer_route(qp, sub_keys, tm=512):
    t = qp.shape[0]
    key_out = pl.BlockSpec((1, N_KEYS, tm), lambda i, h: (h, 0, i))
    key_shape = jax.ShapeDtypeStruct((PEER_HEADS, N_KEYS, t), F32)
    return pl.pallas_call(
        _route_kernel,
        grid=(t // tm, PEER_HEADS),
        in_specs=[pl.BlockSpec((tm, 2 * PEER_HALF), lambda i, h: (i, h)),
                  pl.BlockSpec((1, 2, N_KEYS, PEER_HALF), lambda i, h: (h, 0, 0, 0))],
        out_specs=[key_out, key_out, key_out, key_out,
                   pl.BlockSpec((1, 1, tm), lambda i, h: (h, 0, i))],
        out_shape=[key_shape, key_shape, key_shape, key_shape,
                   jax.ShapeDtypeStruct((PEER_HEADS, 1, t), F32)],
        compiler_params=_params("parallel", "parallel"),
        name="peer_route",
    )(qp, sub_keys)


def _peer_kernel(x_ref, u_ref, v_ref, s1_ref, s2_ref, e1_ref, e2_ref, tau_ref, o_ref, *, n_grp):
    c = pl.program_id(1)

    @pl.when(c == 0)
    def _():
        o_ref[...] = jnp.zeros_like(o_ref)

    hid = lax.dot_general(u_ref[...], x_ref[...], (((1,), (1,)), ((), ())),
                          preferred_element_type=F32)
    sqrt_half = 0.7071067811865476
    w_parts = []
    for g in range(n_grp):
        i = c * n_grp + g
        gate = None
        for h in range(PEER_HEADS):
            s1 = s1_ref[h, pl.ds(i, 1), :]
            e1 = e1_ref[h, pl.ds(i, 1), :]
            active = (s1 + s2_ref[h]) >= tau_ref[h]
            contrib = jnp.where(active, e1 * e2_ref[h], 0.0)
            gate = contrib if gate is None else gate + contrib
        hg = hid[g * N_KEYS:(g + 1) * N_KEYS, :]
        act = 0.5 * hg * (1.0 + lax.erf(hg * sqrt_half))
        w_parts.append((act * gate).astype(BF16))
    w = w_parts[0] if n_grp == 1 else jnp.concatenate(w_parts, axis=0)
    o_ref[...] += lax.dot_general(w, v_ref[...], (((0,), (0,)), ((), ())),
                                  preferred_element_type=F32)


def _peer_mix(xn, u, v, s1, s2, e1, e2, tau, tt=512, n_grp=4):
    t, d = xn.shape
    ec = n_grp * N_KEYS
    once = pl.Buffered(1)
    tok = pl.BlockSpec((PEER_HEADS, N_KEYS, tt), lambda i, c: (0, 0, i), pipeline_mode=once)
    return pl.pallas_call(
        functools.partial(_peer_kernel, n_grp=n_grp),
        grid=(t // tt, N_EXPERTS // ec),
        in_specs=[pl.BlockSpec((tt, d), lambda i, c: (i, 0), pipeline_mode=once),
                  pl.BlockSpec((ec, d), lambda i, c: (c, 0)),
                  pl.BlockSpec((ec, d), lambda i, c: (c, 0)),
                  tok, tok, tok, tok,
                  pl.BlockSpec((PEER_HEADS, 1, tt), lambda i, c: (0, 0, i))],
        out_specs=pl.BlockSpec((tt, d), lambda i, c: (i, 0)),
        out_shape=jax.ShapeDtypeStruct((t, d), F32),
        compiler_params=_params("parallel", "arbitrary"),
        name="peer_mix",
    )(xn, u, v, s1, s2, e1, e2, tau)


def kernel(x, norm_mix, w_in, b_gate, conv_w, w_conv_out, w_attn_out, w_o, norm_ffn,
           w_peer_q, sub_keys, u_emb, v_emb, norm_final):
    batch, seq, d = x.shape
    assert w_in.shape[0] == 1, "single-layer block"
    h = x.reshape(batch * seq, d)
    hn = _rmsnorm(h, norm_mix[0], BF16)
    z = _matmul(hn, w_in[0].astype(BF16), F32)
    bz = _short_conv(z, conv_w[0], batch, seq)
    attn = _moba_attention(z, batch, seq)
    merged = _merge(bz, attn, w_conv_out[0].astype(BF16), w_attn_out[0].astype(BF16),
                    z, b_gate[0])
    h = _matmul_residual(merged, w_o[0].astype(BF16), h)
    hn = _rmsnorm(h, norm_ffn[0], BF16)
    qp = _matmul(hn, w_peer_q[0].astype(BF16), F32)
    s1, s2, e1, e2, tau = _peer_route(qp, sub_keys[0])
    peer = _peer_mix(hn, u_emb[0].astype(BF16), v_emb[0].astype(BF16), s1, s2, e1, e2, tau)
    return _add_rmsnorm(h, peer, norm_final).reshape(batch, seq, d)
```

```python
import functools

import jax
import jax.numpy as jnp
from jax import lax
from jax.experimental import pallas as pl
from jax.experimental.pallas import tpu as pltpu

D_MODEL = 4096
CONV_DIM = 2048
CONV_WIDTH = 3
N_HEADS = 16
HEAD_DIM = 128
ATTN_DIM = N_HEADS * HEAD_DIM
MOBA_BLOCK = 256
MOBA_TOPK = 3
PEER_HEADS = 8
N_KEYS = 128
N_EXPERTS = N_KEYS * N_KEYS
PEER_HALF = 128
PEER_TOPK = 16
EPS = 1e-6

COL_B = 0
COL_C = CONV_DIM
COL_U = 2 * CONV_DIM
COL_Q = 3 * CONV_DIM
COL_K = COL_Q + ATTN_DIM
COL_V = COL_K + ATTN_DIM
COL_GC = COL_V + ATTN_DIM
COL_GA = COL_GC + D_MODEL
IN_COLS = COL_GA + D_MODEL

VMEM_LIMIT_BYTES = 56 * 1024 * 1024

F32 = jnp.float32
BF16 = jnp.bfloat16
NEG_INF = float("-inf")


def _params(*semantics):
    return pltpu.CompilerParams(dimension_semantics=semantics,
                                vmem_limit_bytes=VMEM_LIMIT_BYTES)


def _rmsnorm_kernel(x_ref, g_ref, o_ref):
    x = x_ref[...]
    y = x * lax.rsqrt(jnp.mean(x * x, axis=-1, keepdims=True) + EPS)
    o_ref[...] = (y * g_ref[...]).astype(o_ref.dtype)


def _rmsnorm(x, g, out_dtype, tm=256):
    m, d = x.shape
    return pl.pallas_call(
        _rmsnorm_kernel,
        grid=(m // tm,),
        in_specs=[pl.BlockSpec((tm, d), lambda i: (i, 0)),
                  pl.BlockSpec((1, d), lambda i: (0, 0))],
        out_specs=pl.BlockSpec((tm, d), lambda i: (i, 0)),
        out_shape=jax.ShapeDtypeStruct((m, d), out_dtype),
        compiler_params=_params("parallel"),
        name="rmsnorm",
    )(x, g.reshape(1, d))


def _rmsnorm_t_kernel(x_ref, g_ref, o_ref, ot_ref):
    x = x_ref[...]
    y = x * lax.rsqrt(jnp.mean(x * x, axis=-1, keepdims=True) + EPS) * g_ref[...]
    o_ref[...] = y.astype(o_ref.dtype)
    ot_ref[...] = y.T.astype(ot_ref.dtype)


def _rmsnorm_with_transpose(x, g, tm=256):
    m, d = x.shape
    return pl.pallas_call(
        _rmsnorm_t_kernel,
        grid=(m // tm,),
        in_specs=[pl.BlockSpec((tm, d), lambda i: (i, 0)),
                  pl.BlockSpec((1, d), lambda i: (0, 0))],
        out_specs=[pl.BlockSpec((tm, d), lambda i: (i, 0)),
                   pl.BlockSpec((d, tm), lambda i: (0, i))],
        out_shape=[jax.ShapeDtypeStruct((m, d), BF16),
                   jax.ShapeDtypeStruct((d, m), BF16)],
        compiler_params=_params("parallel"),
        name="rmsnorm_t",
    )(x, g.reshape(1, d))


def _add_rmsnorm_kernel(a_ref, bt_ref, g_ref, o_ref):
    x = a_ref[...] + bt_ref[...].T
    y = x * lax.rsqrt(jnp.mean(x * x, axis=-1, keepdims=True) + EPS)
    o_ref[...] = (y * g_ref[...]).astype(o_ref.dtype)


def _add_rmsnorm(a, bt, g, tm=256):
    m, d = a.shape
    return pl.pallas_call(
        _add_rmsnorm_kernel,
        grid=(m // tm,),
        in_specs=[pl.BlockSpec((tm, d), lambda i: (i, 0)),
                  pl.BlockSpec((d, tm), lambda i: (0, i)),
                  pl.BlockSpec((1, d), lambda i: (0, 0))],
        out_specs=pl.BlockSpec((tm, d), lambda i: (i, 0)),
        out_shape=jax.ShapeDtypeStruct((m, d), F32),
        compiler_params=_params("parallel"),
        name="add_rmsnorm",
    )(a, bt, g.reshape(1, d))


def _matmul_kernel(a_ref, w_ref, o_ref):
    o_ref[...] = jnp.dot(a_ref[...], w_ref[...],
                         preferred_element_type=F32).astype(o_ref.dtype)


def _matmul(a, w, out_dtype, tm=1024, tn=1024):
    m, k = a.shape
    _, n = w.shape
    return pl.pallas_call(
        _matmul_kernel,
        grid=(n // tn, m // tm),
        in_specs=[pl.BlockSpec((tm, k), lambda j, i: (i, 0)),
                  pl.BlockSpec((k, tn), lambda j, i: (0, j))],
        out_specs=pl.BlockSpec((tm, tn), lambda j, i: (i, j)),
        out_shape=jax.ShapeDtypeStruct((m, n), out_dtype),
        compiler_params=_params("parallel", "parallel"),
        name="matmul",
    )(a, w)


def _matmul_residual_kernel(a_ref, w_ref, r_ref, o_ref):
    o_ref[...] = r_ref[...] + jnp.dot(a_ref[...], w_ref[...], preferred_element_type=F32)


def _matmul_residual(a, w, r, tm=1024, tn=1024):
    m, k = a.shape
    _, n = w.shape
    return pl.pallas_call(
        _matmul_residual_kernel,
        grid=(n // tn, m // tm),
        in_specs=[pl.BlockSpec((tm, k), lambda j, i: (i, 0)),
                  pl.BlockSpec((k, tn), lambda j, i: (0, j)),
                  pl.BlockSpec((tm, tn), lambda j, i: (i, j))],
        out_specs=pl.BlockSpec((tm, tn), lambda j, i: (i, j)),
        out_shape=jax.ShapeDtypeStruct((m, n), F32),
        compiler_params=_params("parallel", "parallel"),
        name="matmul_residual",
    )(a, w, r)


def _conv_kernel(b_ref, c_ref, u_ref, w_ref, o_ref):
    z = c_ref[...] * u_ref[...]
    row = lax.broadcasted_iota(jnp.int32, z.shape, 0)
    w = w_ref[...]
    acc = z * w[2:3, :]
    for shift in (1, 2):
        zs = jnp.where(row >= shift, pltpu.roll(z, shift, axis=0), 0.0)
        acc = acc + zs * w[2 - shift:3 - shift, :]
    o_ref[...] = (b_ref[...] * acc).astype(o_ref.dtype)


def _short_conv(z, conv_w, batch, seq, tn=256):
    nb = CONV_DIM // tn
    return pl.pallas_call(
        _conv_kernel,
        grid=(batch, nb),
        in_specs=[pl.BlockSpec((seq, tn), lambda b, j: (b, COL_B // tn + j)),
                  pl.BlockSpec((seq, tn), lambda b, j: (b, COL_C // tn + j)),
                  pl.BlockSpec((seq, tn), lambda b, j: (b, COL_U // tn + j)),
                  pl.BlockSpec((CONV_WIDTH, tn), lambda b, j: (0, j))],
        out_specs=pl.BlockSpec((seq, tn), lambda b, j: (b, j)),
        out_shape=jax.ShapeDtypeStruct((batch * seq, CONV_DIM), BF16),
        compiler_params=_params("parallel", "parallel"),
        name="short_conv",
    )(z, z, z, conv_w)


def _moba_kernel(q_ref, k_ref, v_ref, o_ref, *, n_blk):
    seq = q_ref.shape[0]
    nt_dims = (((1,), (1,)), ((), ()))
    k = k_ref[...]
    kb = k.astype(BF16)
    vb = v_ref[...].astype(BF16)
    q_all = q_ref[...]
    k_mean = jnp.mean(k.reshape(n_blk, MOBA_BLOCK, HEAD_DIM), axis=1)

    gate = lax.dot_general(k_mean, q_all, nt_dims, precision=lax.Precision.HIGHEST,
                           preferred_element_type=F32)
    blk = lax.broadcasted_iota(jnp.int32, (n_blk, seq), 0)
    own = lax.broadcasted_iota(jnp.int32, (n_blk, seq), 1) // MOBA_BLOCK
    past = blk < own
    gate = jnp.where(past, gate, NEG_INF)
    rank = jnp.zeros((n_blk, seq), jnp.int32)
    for jp in range(n_blk - 1):
        gj = gate[jp:jp + 1, :]
        ahead = (gj > gate) | ((gj == gate) & (blk > jp))
        rank = rank + ahead.astype(jnp.int32)
    sel = jnp.where((rank < MOBA_TOPK) & past, 1.0, 0.0).astype(BF16)
    expand = jnp.where(lax.broadcasted_iota(jnp.int32, (n_blk, seq), 1) // MOBA_BLOCK == blk,
                       1.0, 0.0).astype(BF16)

    row = lax.broadcasted_iota(jnp.int32, (MOBA_BLOCK, MOBA_BLOCK), 0)
    col = lax.broadcasted_iota(jnp.int32, (MOBA_BLOCK, MOBA_BLOCK), 1)
    causal = col <= row
    q_scale = HEAD_DIM ** -0.5 * 1.4426950408889634
    for i in range(n_blk):
        lo, nk = i * MOBA_BLOCK, (i + 1) * MOBA_BLOCK
        q = (q_all[lo:nk] * q_scale).astype(BF16)
        s = lax.dot_general(q, kb[:nk], nt_dims, preferred_element_type=F32)
        s_own = jnp.where(causal, s[:, lo:nk], NEG_INF)
        if i > 0:
            keep = lax.dot_general(sel[:, lo:nk], expand[:, :lo], (((0,), (0,)), ((), ())),
                                   preferred_element_type=F32)
            s = jnp.concatenate([jnp.where(keep > 0.5, s[:, :lo], NEG_INF), s_own], axis=1)
        else:
            s = s_own
        m = jnp.max(s, axis=-1, keepdims=True)
        p = jnp.exp2(s - m)
        l = jnp.sum(p, axis=-1, keepdims=True)
        out = jnp.dot(p.astype(BF16), vb[:nk], preferred_element_type=F32) / l
        o_ref[lo:nk, :] = out.astype(o_ref.dtype)


def _moba_attention(z, batch, seq):
    n_blk = seq // MOBA_BLOCK
    spec = lambda off: pl.BlockSpec((seq, HEAD_DIM), lambda b, h: (b, off // HEAD_DIM + h))
    return pl.pallas_call(
        functools.partial(_moba_kernel, n_blk=n_blk),
        grid=(batch, N_HEADS),
        in_specs=[spec(COL_Q), spec(COL_K), spec(COL_V)],
        out_specs=pl.BlockSpec((seq, HEAD_DIM), lambda b, h: (b, h)),
        out_shape=jax.ShapeDtypeStruct((batch * seq, ATTN_DIM), BF16),
        compiler_params=_params("parallel", "parallel"),
        name="moba_attention",
    )(z, z, z)


def _merge_kernel(bz_ref, at_ref, wc_ref, wa_ref, gc_ref, ga_ref, bc_ref, ba_ref, o_ref):
    y_conv = jnp.dot(bz_ref[...], wc_ref[...], preferred_element_type=F32)
    y_attn = jnp.dot(at_ref[...], wa_ref[...], preferred_element_type=F32)
    merged = (jax.nn.sigmoid(gc_ref[...] + bc_ref[...]) * y_conv
              + jax.nn.sigmoid(ga_ref[...] + ba_ref[...]) * y_attn)
    o_ref[...] = merged.astype(o_ref.dtype)


def _merge(bz, attn, w_conv_out, w_attn_out, z, b_gate, tm=512, tn=1024):
    m = bz.shape[0]
    nb = D_MODEL // tn
    bias = b_gate.reshape(1, 2 * D_MODEL)
    return pl.pallas_call(
        _merge_kernel,
        grid=(nb, m // tm),
        in_specs=[pl.BlockSpec((tm, CONV_DIM), lambda j, i: (i, 0)),
                  pl.BlockSpec((tm, ATTN_DIM), lambda j, i: (i, 0)),
                  pl.BlockSpec((CONV_DIM, tn), lambda j, i: (0, j)),
                  pl.BlockSpec((ATTN_DIM, tn), lambda j, i: (0, j)),
                  pl.BlockSpec((tm, tn), lambda j, i: (i, COL_GC // tn + j)),
                  pl.BlockSpec((tm, tn), lambda j, i: (i, COL_GA // tn + j)),
                  pl.BlockSpec((1, tn), lambda j, i: (0, j)),
                  pl.BlockSpec((1, tn), lambda j, i: (0, nb + j))],
        out_specs=pl.BlockSpec((tm, tn), lambda j, i: (i, j)),
        out_shape=jax.ShapeDtypeStruct((m, D_MODEL), BF16),
        compiler_params=_params("parallel", "parallel"),
        name="merge",
    )(bz, attn, w_conv_out, w_attn_out, z, z, bias, bias)


def _extract_top(s, n):
    rows = s.shape[0]
    idx = lax.broadcasted_iota(jnp.int32, s.shape, 0)
    vals, firsts = [], []
    for _ in range(n):
        m = jnp.max(s, axis=0, keepdims=True)
        first = jnp.min(jnp.where(s == m, idx, rows), axis=0, keepdims=True)
        s = jnp.where(idx == first, NEG_INF, s)
        vals.append(m)
        firsts.append(first)
    return vals, firsts, s


def _route_kernel(q_ref, sk_ref, thr_ref, s2_ref, e1_ref, e2_ref):
    q = q_ref[...]
    tm = q.shape[0]
    dims = (((1,), (1,)), ((), ()))
    s1 = lax.dot_general(sk_ref[0, 0], q[:, :PEER_HALF], dims,
                         precision=lax.Precision.HIGHEST, preferred_element_type=F32)
    s2 = lax.dot_general(sk_ref[0, 1], q[:, PEER_HALF:], dims,
                         precision=lax.Precision.HIGHEST, preferred_element_type=F32)
    t1, first1, _ = _extract_top(s1, PEER_TOPK)
    t2, _, _ = _extract_top(s2, PEER_TOPK)
    t2 = jnp.concatenate(t2, axis=0)
    sub = 8
    row = lax.broadcasted_iota(jnp.int32, (sub, tm), 0)
    pieces = [t1[0] + t2]
    for a in range(1, sub):
        piece = t1[a] + t2[:sub]
        n_valid = PEER_TOPK // (a + 1)
        pieces.append(piece if n_valid >= sub else jnp.where(row < n_valid, piece, NEG_INF))
    pieces.append(jnp.concatenate(t1[sub:], axis=0) + t2[0:1])
    cand = jnp.concatenate(pieces, axis=0)
    best, _, rest = _extract_top(cand, PEER_TOPK)
    picked = rest != cand
    z = jnp.ones_like(best[0])
    for r in range(1, PEER_TOPK):
        z = z + jnp.exp(best[r] - best[0])
    inf = float("inf")
    theta = [jnp.min(jnp.where(picked[:PEER_TOPK], t2, inf), axis=0, keepdims=True)]
    for a in range(1, sub):
        lo = PEER_TOPK + (a - 1) * sub
        theta.append(jnp.min(jnp.where(picked[lo:lo + sub], t2[:sub], inf), axis=0, keepdims=True))
    lo = PEER_TOPK + (sub - 1) * sub
    tail = jnp.where(picked[lo:lo + sub], t2[0:1], inf)
    theta += [tail[a:a + 1] for a in range(sub)]
    idx = lax.broadcasted_iota(jnp.int32, s1.shape, 0)
    thr = jnp.full(s1.shape, inf, F32)
    for a in range(PEER_TOPK):
        thr = jnp.where(idx == first1[a], theta[a], thr)
    thr_ref[0] = thr
    s2_ref[0] = s2
    e1_ref[0] = jnp.exp(s1 - t1[0]) / z
    e2_ref[0] = jnp.exp(s2 - t2[0:1, :])


def _peer_route(qp, sub_keys, tm=512):
    t = qp.shape[0]
    key_out = pl.BlockSpec((1, N_KEYS, tm), lambda i, h: (h, 0, i))
    key_shape = jax.ShapeDtypeStruct((PEER_HEADS, N_KEYS, t), F32)
    return pl.pallas_call(
        _route_kernel,
        grid=(t // tm, PEER_HEADS),
        in_specs=[pl.BlockSpec((tm, 2 * PEER_HALF), lambda i, h: (i, h)),
                  pl.BlockSpec((1, 2, N_KEYS, PEER_HALF), lambda i, h: (h, 0, 0, 0))],
        out_specs=[key_out, key_out, key_out, key_out],
        out_shape=[key_shape, key_shape, key_shape, key_shape],
        compiler_params=_params("parallel", "parallel"),
        name="peer_route",
    )(qp, sub_keys)


def _peer_kernel(xt_ref, u_ref, vt_ref, thr_ref, s2_ref, e1_ref, e2_ref, o_ref, *, n_grp):
    c = pl.program_id(1)

    @pl.when(c == 0)
    def _():
        o_ref[...] = jnp.zeros_like(o_ref)

    hid = jnp.dot(u_ref[...], xt_ref[...], preferred_element_type=F32)
    sqrt_half = 0.7071067811865476
    w_parts = []
    for g in range(n_grp):
        i = c * n_grp + g
        gate = None
        for h in range(PEER_HEADS):
            thr = thr_ref[h, pl.ds(i, 1), :]
            e1 = e1_ref[h, pl.ds(i, 1), :]
            contrib = jnp.where(s2_ref[h] >= thr, e1 * e2_ref[h], 0.0)
            gate = contrib if gate is None else gate + contrib
        hg = hid[g * N_KEYS:(g + 1) * N_KEYS, :]
        act = 0.5 * hg * (1.0 + lax.erf(hg * sqrt_half))
        w_parts.append((act * gate).astype(BF16))
    w = w_parts[0] if n_grp == 1 else jnp.concatenate(w_parts, axis=0)
    o_ref[...] += jnp.dot(vt_ref[...], w, preferred_element_type=F32)


def _peer_mix(xnt, u, vt, thr, s2, e1, e2, tt=512, n_grp=4):
    d, t = xnt.shape
    ec = n_grp * N_KEYS
    once = pl.Buffered(1)
    tok = pl.BlockSpec((PEER_HEADS, N_KEYS, tt), lambda i, c: (0, 0, i), pipeline_mode=once)
    return pl.pallas_call(
        functools.partial(_peer_kernel, n_grp=n_grp),
        grid=(t // tt, N_EXPERTS // ec),
        in_specs=[pl.BlockSpec((d, tt), lambda i, c: (0, i), pipeline_mode=once),
                  pl.BlockSpec((ec, d), lambda i, c: (c, 0)),
                  pl.BlockSpec((d, ec), lambda i, c: (0, c)),
                  tok, tok, tok, tok],
        out_specs=pl.BlockSpec((d, tt), lambda i, c: (0, i)),
        out_shape=jax.ShapeDtypeStruct((d, t), F32),
        compiler_params=_params("parallel", "arbitrary"),
        name="peer_mix",
    )(xnt, u, vt, thr, s2, e1, e2)


def kernel(x, norm_mix, w_in, b_gate, conv_w, w_conv_out, w_attn_out, w_o, norm_ffn,
           w_peer_q, sub_keys, u_emb, v_emb, norm_final):
    batch, seq, d = x.shape
    assert w_in.shape[0] == 1, "single-layer block"
    h = x.reshape(batch * seq, d)
    hn = _rmsnorm(h, norm_mix[0], BF16)
    z = _matmul(hn, w_in[0].astype(BF16), F32)
    bz = _short_conv(z, conv_w[0], batch, seq)
    attn = _moba_attention(z, batch, seq)
    merged = _merge(bz, attn, w_conv_out[0].astype(BF16), w_attn_out[0].astype(BF16),
                    z, b_gate[0])
    h = _matmul_residual(merged, w_o[0].astype(BF16), h)
    hn, hnt = _rmsnorm_with_transpose(h, norm_ffn[0])
    qp = _matmul(hn, w_peer_q[0].astype(BF16), F32)
    thr, s2, e1, e2 = _peer_route(qp, sub_keys[0])
    peer_t = _peer_mix(hnt, u_emb[0].astype(BF16), v_emb[0].T.astype(BF16), thr, s2, e1, e2)
    return _add_rmsnorm(h, peer_t, norm_final).reshape(batch, seq, d)
```

```python
import functools

import jax
import jax.numpy as jnp
from jax import lax
from jax.experimental import pallas as pl
from jax.experimental.pallas import tpu as pltpu

D_MODEL = 4096
CONV_DIM = 2048
CONV_WIDTH = 3
N_HEADS = 16
HEAD_DIM = 128
ATTN_DIM = N_HEADS * HEAD_DIM
MOBA_BLOCK = 256
MOBA_TOPK = 3
PEER_HEADS = 8
N_KEYS = 128
N_EXPERTS = N_KEYS * N_KEYS
PEER_HALF = 128
PEER_TOPK = 16
EPS = 1e-6

COL_B = 0
COL_C = CONV_DIM
COL_U = 2 * CONV_DIM
COL_Q = 3 * CONV_DIM
COL_K = COL_Q + ATTN_DIM
COL_V = COL_K + ATTN_DIM
COL_GC = COL_V + ATTN_DIM
COL_GA = COL_GC + D_MODEL
IN_COLS = COL_GA + D_MODEL

VMEM_LIMIT_BYTES = 56 * 1024 * 1024

F32 = jnp.float32
BF16 = jnp.bfloat16
NEG_INF = float("-inf")


def _params(*semantics):
    return pltpu.CompilerParams(dimension_semantics=semantics,
                                vmem_limit_bytes=VMEM_LIMIT_BYTES)


def _rmsnorm_kernel(x_ref, g_ref, o_ref):
    x = x_ref[...]
    y = x * lax.rsqrt(jnp.mean(x * x, axis=-1, keepdims=True) + EPS)
    o_ref[...] = (y * g_ref[...]).astype(o_ref.dtype)


def _rmsnorm(x, g, out_dtype, tm=256):
    m, d = x.shape
    return pl.pallas_call(
        _rmsnorm_kernel,
        grid=(m // tm,),
        in_specs=[pl.BlockSpec((tm, d), lambda i: (i, 0)),
                  pl.BlockSpec((1, d), lambda i: (0, 0))],
        out_specs=pl.BlockSpec((tm, d), lambda i: (i, 0)),
        out_shape=jax.ShapeDtypeStruct((m, d), out_dtype),
        compiler_params=_params("parallel"),
        name="rmsnorm",
    )(x, g.reshape(1, d))


def _rmsnorm_t_kernel(x_ref, g_ref, o_ref, ot_ref):
    x = x_ref[...]
    y = x * lax.rsqrt(jnp.mean(x * x, axis=-1, keepdims=True) + EPS) * g_ref[...]
    o_ref[...] = y.astype(o_ref.dtype)
    ot_ref[...] = y.T.astype(ot_ref.dtype)


def _rmsnorm_with_transpose(x, g, tm=256):
    m, d = x.shape
    return pl.pallas_call(
        _rmsnorm_t_kernel,
        grid=(m // tm,),
        in_specs=[pl.BlockSpec((tm, d), lambda i: (i, 0)),
                  pl.BlockSpec((1, d), lambda i: (0, 0))],
        out_specs=[pl.BlockSpec((tm, d), lambda i: (i, 0)),
                   pl.BlockSpec((d, tm), lambda i: (0, i))],
        out_shape=[jax.ShapeDtypeStruct((m, d), BF16),
                   jax.ShapeDtypeStruct((d, m), BF16)],
        compiler_params=_params("parallel"),
        name="rmsnorm_t",
    )(x, g.reshape(1, d))


def _add_rmsnorm_kernel(a_ref, bt_ref, g_ref, o_ref):
    x = a_ref[...] + bt_ref[...].T
    y = x * lax.rsqrt(jnp.mean(x * x, axis=-1, keepdims=True) + EPS)
    o_ref[...] = (y * g_ref[...]).astype(o_ref.dtype)


def _add_rmsnorm(a, bt, g, tm=256):
    m, d = a.shape
    return pl.pallas_call(
        _add_rmsnorm_kernel,
        grid=(m // tm,),
        in_specs=[pl.BlockSpec((tm, d), lambda i: (i, 0)),
                  pl.BlockSpec((d, tm), lambda i: (0, i)),
                  pl.BlockSpec((1, d), lambda i: (0, 0))],
        out_specs=pl.BlockSpec((tm, d), lambda i: (i, 0)),
        out_shape=jax.ShapeDtypeStruct((m, d), F32),
        compiler_params=_params("parallel"),
        name="add_rmsnorm",
    )(a, bt, g.reshape(1, d))


def _matmul_kernel(a_ref, w_ref, o_ref):
    o_ref[...] = jnp.dot(a_ref[...], w_ref[...],
                         preferred_element_type=F32).astype(o_ref.dtype)


def _matmul(a, w, out_dtype, tm=1024, tn=1024):
    m, k = a.shape
    _, n = w.shape
    return pl.pallas_call(
        _matmul_kernel,
        grid=(n // tn, m // tm),
        in_specs=[pl.BlockSpec((tm, k), lambda j, i: (i, 0)),
                  pl.BlockSpec((k, tn), lambda j, i: (0, j))],
        out_specs=pl.BlockSpec((tm, tn), lambda j, i: (i, j)),
        out_shape=jax.ShapeDtypeStruct((m, n), out_dtype),
        compiler_params=_params("parallel", "parallel"),
        name="matmul",
    )(a, w)


def _matmul_residual_kernel(a_ref, w_ref, r_ref, o_ref):
    o_ref[...] = r_ref[...] + jnp.dot(a_ref[...], w_ref[...], preferred_element_type=F32)


def _matmul_residual(a, w, r, tm=1024, tn=1024):
    m, k = a.shape
    _, n = w.shape
    return pl.pallas_call(
        _matmul_residual_kernel,
        grid=(n // tn, m // tm),
        in_specs=[pl.BlockSpec((tm, k), lambda j, i: (i, 0)),
                  pl.BlockSpec((k, tn), lambda j, i: (0, j)),
                  pl.BlockSpec((tm, tn), lambda j, i: (i, j))],
        out_specs=pl.BlockSpec((tm, tn), lambda j, i: (i, j)),
        out_shape=jax.ShapeDtypeStruct((m, n), F32),
        compiler_params=_params("parallel", "parallel"),
        name="matmul_residual",
    )(a, w, r)


def _conv_kernel(b_ref, c_ref, u_ref, w_ref, o_ref):
    z = c_ref[...] * u_ref[...]
    row = lax.broadcasted_iota(jnp.int32, z.shape, 0)
    w = w_ref[...]
    acc = z * w[2:3, :]
    for shift in (1, 2):
        zs = jnp.where(row >= shift, pltpu.roll(z, shift, axis=0), 0.0)
        acc = acc + zs * w[2 - shift:3 - shift, :]
    o_ref[...] = (b_ref[...] * acc).astype(o_ref.dtype)


def _short_conv(z, conv_w, batch, seq, tn=256):
    nb = CONV_DIM // tn
    return pl.pallas_call(
        _conv_kernel,
        grid=(batch, nb),
        in_specs=[pl.BlockSpec((seq, tn), lambda b, j: (b, COL_B // tn + j)),
                  pl.BlockSpec((seq, tn), lambda b, j: (b, COL_C // tn + j)),
                  pl.BlockSpec((seq, tn), lambda b, j: (b, COL_U // tn + j)),
                  pl.BlockSpec((CONV_WIDTH, tn), lambda b, j: (0, j))],
        out_specs=pl.BlockSpec((seq, tn), lambda b, j: (b, j)),
        out_shape=jax.ShapeDtypeStruct((batch * seq, CONV_DIM), BF16),
        compiler_params=_params("parallel", "parallel"),
        name="short_conv",
    )(z, z, z, conv_w)


def _moba_kernel(q_ref, k_ref, v_ref, o_ref, *, n_blk):
    seq = q_ref.shape[0]
    nt_dims = (((1,), (1,)), ((), ()))
    k = k_ref[...]
    kb = k.astype(BF16)
    vb = v_ref[...].astype(BF16)
    q_all = q_ref[...]
    k_mean = jnp.mean(k.reshape(n_blk, MOBA_BLOCK, HEAD_DIM), axis=1)

    gate = lax.dot_general(k_mean, q_all, nt_dims, precision=lax.Precision.HIGHEST,
                           preferred_element_type=F32)
    blk = lax.broadcasted_iota(jnp.int32, (n_blk, seq), 0)
    own = lax.broadcasted_iota(jnp.int32, (n_blk, seq), 1) // MOBA_BLOCK
    past = blk < own
    gate = jnp.where(past, gate, NEG_INF)
    rank = jnp.zeros((n_blk, seq), jnp.int32)
    for jp in range(n_blk - 1):
        gj = gate[jp:jp + 1, :]
        ahead = (gj > gate) | ((gj == gate) & (blk > jp))
        rank = rank + ahead.astype(jnp.int32)
    sel = jnp.where((rank < MOBA_TOPK) & past, 1.0, 0.0).astype(BF16)
    expand = jnp.where(lax.broadcasted_iota(jnp.int32, (n_blk, seq), 1) // MOBA_BLOCK == blk,
                       1.0, 0.0).astype(BF16)

    row = lax.broadcasted_iota(jnp.int32, (MOBA_BLOCK, MOBA_BLOCK), 0)
    col = lax.broadcasted_iota(jnp.int32, (MOBA_BLOCK, MOBA_BLOCK), 1)
    causal = col <= row
    q_scale = HEAD_DIM ** -0.5 * 1.4426950408889634
    for i in range(n_blk):
        lo, nk = i * MOBA_BLOCK, (i + 1) * MOBA_BLOCK
        q = (q_all[lo:nk] * q_scale).astype(BF16)
        s = lax.dot_general(q, kb[:nk], nt_dims, preferred_element_type=F32)
        s_own = jnp.where(causal, s[:, lo:nk], NEG_INF)
        if i > 0:
            keep = lax.dot_general(sel[:, lo:nk], expand[:, :lo], (((0,), (0,)), ((), ())),
                                   preferred_element_type=F32)
            s = jnp.concatenate([jnp.where(keep > 0.5, s[:, :lo], NEG_INF), s_own], axis=1)
        else:
            s = s_own
        m = jnp.max(s, axis=-1, keepdims=True)
        p = jnp.exp2(s - m)
        l = jnp.sum(p, axis=-1, keepdims=True)
        out = jnp.dot(p.astype(BF16), vb[:nk], preferred_element_type=F32) / l
        o_ref[lo:nk, :] = out.astype(o_ref.dtype)


def _moba_attention(z, batch, seq):
    n_blk = seq // MOBA_BLOCK
    spec = lambda off: pl.BlockSpec((seq, HEAD_DIM), lambda b, h: (b, off // HEAD_DIM + h))
    return pl.pallas_call(
        functools.partial(_moba_kernel, n_blk=n_blk),
        grid=(batch, N_HEADS),
        in_specs=[spec(COL_Q), spec(COL_K), spec(COL_V)],
        out_specs=pl.BlockSpec((seq, HEAD_DIM), lambda b, h: (b, h)),
        out_shape=jax.ShapeDtypeStruct((batch * seq, ATTN_DIM), BF16),
        compiler_params=_params("parallel", "parallel"),
        name="moba_attention",
    )(z, z, z)


def _merge_kernel(bz_ref, at_ref, wc_ref, wa_ref, gc_ref, ga_ref, bc_ref, ba_ref, o_ref):
    y_conv = jnp.dot(bz_ref[...], wc_ref[...], preferred_element_type=F32)
    y_attn = jnp.dot(at_ref[...], wa_ref[...], preferred_element_type=F32)
    merged = (jax.nn.sigmoid(gc_ref[...] + bc_ref[...]) * y_conv
              + jax.nn.sigmoid(ga_ref[...] + ba_ref[...]) * y_attn)
    o_ref[...] = merged.astype(o_ref.dtype)


def _merge(bz, attn, w_conv_out, w_attn_out, z, b_gate, tm=512, tn=1024):
    m = bz.shape[0]
    nb = D_MODEL // tn
    bias = b_gate.reshape(1, 2 * D_MODEL)
    return pl.pallas_call(
        _merge_kernel,
        grid=(nb, m // tm),
        in_specs=[pl.BlockSpec((tm, CONV_DIM), lambda j, i: (i, 0)),
                  pl.BlockSpec((tm, ATTN_DIM), lambda j, i: (i, 0)),
                  pl.BlockSpec((CONV_DIM, tn), lambda j, i: (0, j)),
                  pl.BlockSpec((ATTN_DIM, tn), lambda j, i: (0, j)),
                  pl.BlockSpec((tm, tn), lambda j, i: (i, COL_GC // tn + j)),
                  pl.BlockSpec((tm, tn), lambda j, i: (i, COL_GA // tn + j)),
                  pl.BlockSpec((1, tn), lambda j, i: (0, j)),
                  pl.BlockSpec((1, tn), lambda j, i: (0, nb + j))],
        out_specs=pl.BlockSpec((tm, tn), lambda j, i: (i, j)),
        out_shape=jax.ShapeDtypeStruct((m, D_MODEL), BF16),
        compiler_params=_params("parallel", "parallel"),
        name="merge",
    )(bz, attn, w_conv_out, w_attn_out, z, z, bias, bias)


def _extract_top(s, n):
    rows = s.shape[0]
    idx = lax.broadcasted_iota(jnp.int32, s.shape, 0)
    vals, firsts = [], []
    for _ in range(n):
        m = jnp.max(s, axis=0, keepdims=True)
        first = jnp.min(jnp.where(s == m, idx, rows), axis=0, keepdims=True)
        s = jnp.where(idx == first, NEG_INF, s)
        vals.append(m)
        firsts.append(first)
    return vals, firsts, s


def _route_kernel(q_ref, sk_ref, n1_ref, rank2_ref, e1_ref, e2_ref):
    q = q_ref[...]
    tm = q.shape[0]
    dims = (((1,), (1,)), ((), ()))
    s1 = lax.dot_general(sk_ref[0, 0], q[:, :PEER_HALF], dims,
                         precision=lax.Precision.HIGHEST, preferred_element_type=F32)
    s2 = lax.dot_general(sk_ref[0, 1], q[:, PEER_HALF:], dims,
                         precision=lax.Precision.HIGHEST, preferred_element_type=F32)
    t1, first1, _ = _extract_top(s1, PEER_TOPK)
    t2, first2, _ = _extract_top(s2, PEER_TOPK)
    t2 = jnp.concatenate(t2, axis=0)
    sub = 8
    row = lax.broadcasted_iota(jnp.int32, (sub, tm), 0)
    pieces = [t1[0] + t2]
    for a in range(1, sub):
        piece = t1[a] + t2[:sub]
        n_valid = PEER_TOPK // (a + 1)
        pieces.append(piece if n_valid >= sub else jnp.where(row < n_valid, piece, NEG_INF))
    pieces.append(jnp.concatenate(t1[sub:], axis=0) + t2[0:1])
    cand = jnp.concatenate(pieces, axis=0)
    best, _, rest = _extract_top(cand, PEER_TOPK)
    picked = jnp.where(rest != cand, 1.0, 0.0)
    z = jnp.ones_like(best[0])
    for r in range(1, PEER_TOPK):
        z = z + jnp.exp(best[r] - best[0])
    count = [jnp.sum(picked[:PEER_TOPK], axis=0, keepdims=True)]
    for a in range(1, sub):
        lo = PEER_TOPK + (a - 1) * sub
        count.append(jnp.sum(picked[lo:lo + sub], axis=0, keepdims=True))
    lo = PEER_TOPK + (sub - 1) * sub
    count += [picked[lo + a:lo + a + 1] for a in range(sub)]
    idx = lax.broadcasted_iota(jnp.int32, s1.shape, 0)
    n1 = jnp.zeros(s1.shape, F32)
    rank2 = jnp.full(s1.shape, float(PEER_TOPK), F32)
    for a in range(PEER_TOPK):
        n1 = jnp.where(idx == first1[a], count[a], n1)
        rank2 = jnp.where(idx == first2[a], float(a), rank2)
    n1_ref[0] = n1
    rank2_ref[0] = rank2.astype(rank2_ref.dtype)
    e1_ref[0] = jnp.exp(s1 - t1[0]) / z
    e2_ref[0] = jnp.exp(s2 - t2[0:1, :]).astype(e2_ref.dtype)


def _peer_route(qp, sub_keys, tm=512):
    t = qp.shape[0]
    key_out = pl.BlockSpec((1, N_KEYS, tm), lambda i, h: (h, 0, i))
    key_shape = lambda dtype: jax.ShapeDtypeStruct((PEER_HEADS, N_KEYS, t), dtype)
    return pl.pallas_call(
        _route_kernel,
        grid=(t // tm, PEER_HEADS),
        in_specs=[pl.BlockSpec((tm, 2 * PEER_HALF), lambda i, h: (i, h)),
                  pl.BlockSpec((1, 2, N_KEYS, PEER_HALF), lambda i, h: (h, 0, 0, 0))],
        out_specs=[key_out, key_out, key_out, key_out],
        out_shape=[key_shape(F32), key_shape(BF16), key_shape(F32), key_shape(BF16)],
        compiler_params=_params("parallel", "parallel"),
        name="peer_route",
    )(qp, sub_keys)


def _peer_kernel(xt_ref, u_ref, vt_ref, n1_ref, rank2_ref, e1_ref, e2_ref, o_ref, *, n_grp):
    c = pl.program_id(1)

    @pl.when(c == 0)
    def _():
        o_ref[...] = jnp.zeros_like(o_ref)

    hid = jnp.dot(u_ref[...], xt_ref[...], preferred_element_type=F32)
    sqrt_half = 0.7071067811865476
    w_parts = []
    for g in range(n_grp):
        i = c * n_grp + g
        gate = None
        for h in range(PEER_HEADS):
            n1 = n1_ref[h, pl.ds(i, 1), :].astype(BF16)
            e1 = e1_ref[h, pl.ds(i, 1), :].astype(BF16)
            contrib = jnp.where(rank2_ref[h] < n1, e1 * e2_ref[h], jnp.zeros((), BF16))
            gate = contrib if gate is None else gate + contrib
        hg = hid[g * N_KEYS:(g + 1) * N_KEYS, :]
        act = 0.5 * hg * (1.0 + lax.erf(hg * sqrt_half))
        w_parts.append(act.astype(BF16) * gate)
    w = w_parts[0] if n_grp == 1 else jnp.concatenate(w_parts, axis=0)
    o_ref[...] += jnp.dot(vt_ref[...], w, preferred_element_type=F32)


def _peer_mix(xnt, u, vt, n1, rank2, e1, e2, tt=512, n_grp=4):
    d, t = xnt.shape
    ec = n_grp * N_KEYS
    once = pl.Buffered(1)
    tok = pl.BlockSpec((PEER_HEADS, N_KEYS, tt), lambda i, c: (0, 0, i), pipeline_mode=once)
    return pl.pallas_call(
        functools.partial(_peer_kernel, n_grp=n_grp),
        grid=(t // tt, N_EXPERTS // ec),
        in_specs=[pl.BlockSpec((d, tt), lambda i, c: (0, i), pipeline_mode=once),
                  pl.BlockSpec((ec, d), lambda i, c: (c, 0)),
                  pl.BlockSpec((d, ec), lambda i, c: (0, c)),
                  tok, tok, tok, tok],
        out_specs=pl.BlockSpec((d, tt), lambda i, c: (0, i)),
        out_shape=jax.ShapeDtypeStruct((d, t), F32),
        compiler_params=_params("parallel", "arbitrary"),
        name="peer_mix",
    )(xnt, u, vt, n1, rank2, e1, e2)


def kernel(x, norm_mix, w_in, b_gate, conv_w, w_conv_out, w_attn_out, w_o, norm_ffn,
           w_peer_q, sub_keys, u_emb, v_emb, norm_final):
    batch, seq, d = x.shape
    assert w_in.shape[0] == 1, "single-layer block"
    h = x.reshape(batch * seq, d)
    hn = _rmsnorm(h, norm_mix[0], BF16)
    z = _matmul(hn, w_in[0].astype(BF16), F32)
    bz = _short_conv(z, conv_w[0], batch, seq)
    attn = _moba_attention(z, batch, seq)
    merged = _merge(bz, attn, w_conv_out[0].astype(BF16), w_attn_out[0].astype(BF16),
                    z, b_gate[0])
    h = _matmul_residual(merged, w_o[0].astype(BF16), h)
    hn, hnt = _rmsnorm_with_transpose(h, norm_ffn[0])
    qp = _matmul(hn, w_peer_q[0].astype(BF16), F32)
    n1, rank2, e1, e2 = _peer_route(qp, sub_keys[0])
    peer_t = _peer_mix(hnt, u_emb[0].astype(BF16), v_emb[0].T.astype(BF16), n1, rank2, e1, e2)
    return _add_rmsnorm(h, peer_t, norm_final).reshape(batch, seq, d)
```

```python
import functools

import jax
import jax.numpy as jnp
from jax import lax
from jax.experimental import pallas as pl
from jax.experimental.pallas import tpu as pltpu

D_MODEL = 4096
CONV_DIM = 2048
CONV_WIDTH = 3
N_HEADS = 16
HEAD_DIM = 128
ATTN_DIM = N_HEADS * HEAD_DIM
MOBA_BLOCK = 256
MOBA_TOPK = 3
PEER_HEADS = 8
N_KEYS = 128
N_EXPERTS = N_KEYS * N_KEYS
PEER_HALF = 128
PEER_TOPK = 16
EPS = 1e-6

COL_B = 0
COL_C = CONV_DIM
COL_U = 2 * CONV_DIM
COL_Q = 3 * CONV_DIM
COL_K = COL_Q + ATTN_DIM
COL_V = COL_K + ATTN_DIM
COL_GC = COL_V + ATTN_DIM
COL_GA = COL_GC + D_MODEL
IN_COLS = COL_GA + D_MODEL

VMEM_LIMIT_BYTES = 56 * 1024 * 1024

F32 = jnp.float32
BF16 = jnp.bfloat16
NEG_INF = float("-inf")


def _params(*semantics):
    return pltpu.CompilerParams(dimension_semantics=semantics,
                                vmem_limit_bytes=VMEM_LIMIT_BYTES)


def _rmsnorm_kernel(x_ref, g_ref, o_ref):
    x = x_ref[...]
    y = x * lax.rsqrt(jnp.mean(x * x, axis=-1, keepdims=True) + EPS)
    o_ref[...] = (y * g_ref[...]).astype(o_ref.dtype)


def _rmsnorm(x, g, out_dtype, tm=256):
    m, d = x.shape
    return pl.pallas_call(
        _rmsnorm_kernel,
        grid=(m // tm,),
        in_specs=[pl.BlockSpec((tm, d), lambda i: (i, 0)),
                  pl.BlockSpec((1, d), lambda i: (0, 0))],
        out_specs=pl.BlockSpec((tm, d), lambda i: (i, 0)),
        out_shape=jax.ShapeDtypeStruct((m, d), out_dtype),
        compiler_params=_params("parallel"),
        name="rmsnorm",
    )(x, g.reshape(1, d))


def _rmsnorm_t_kernel(x_ref, g_ref, o_ref, ot_ref):
    x = x_ref[...]
    y = x * lax.rsqrt(jnp.mean(x * x, axis=-1, keepdims=True) + EPS) * g_ref[...]
    o_ref[...] = y.astype(o_ref.dtype)
    ot_ref[...] = y.T.astype(ot_ref.dtype)


def _rmsnorm_with_transpose(x, g, tm=256):
    m, d = x.shape
    return pl.pallas_call(
        _rmsnorm_t_kernel,
        grid=(m // tm,),
        in_specs=[pl.BlockSpec((tm, d), lambda i: (i, 0)),
                  pl.BlockSpec((1, d), lambda i: (0, 0))],
        out_specs=[pl.BlockSpec((tm, d), lambda i: (i, 0)),
                   pl.BlockSpec((d, tm), lambda i: (0, i))],
        out_shape=[jax.ShapeDtypeStruct((m, d), BF16),
                   jax.ShapeDtypeStruct((d, m), BF16)],
        compiler_params=_params("parallel"),
        name="rmsnorm_t",
    )(x, g.reshape(1, d))


def _add_rmsnorm_kernel(a_ref, bt_ref, g_ref, o_ref):
    x = a_ref[...] + bt_ref[...].T
    y = x * lax.rsqrt(jnp.mean(x * x, axis=-1, keepdims=True) + EPS)
    o_ref[...] = (y * g_ref[...]).astype(o_ref.dtype)


def _add_rmsnorm(a, bt, g, tm=256):
    m, d = a.shape
    return pl.pallas_call(
        _add_rmsnorm_kernel,
        grid=(m // tm,),
        in_specs=[pl.BlockSpec((tm, d), lambda i: (i, 0)),
                  pl.BlockSpec((d, tm), lambda i: (0, i)),
                  pl.BlockSpec((1, d), lambda i: (0, 0))],
        out_specs=pl.BlockSpec((tm, d), lambda i: (i, 0)),
        out_shape=jax.ShapeDtypeStruct((m, d), F32),
        compiler_params=_params("parallel"),
        name="add_rmsnorm",
    )(a, bt, g.reshape(1, d))


def _matmul_kernel(a_ref, w_ref, o_ref):
    o_ref[...] = jnp.dot(a_ref[...], w_ref[...],
                         preferred_element_type=F32).astype(o_ref.dtype)


def _matmul(a, w, out_dtype, tm=1024, tn=1024):
    m, k = a.shape
    _, n = w.shape
    return pl.pallas_call(
        _matmul_kernel,
        grid=(n // tn, m // tm),
        in_specs=[pl.BlockSpec((tm, k), lambda j, i: (i, 0)),
                  pl.BlockSpec((k, tn), lambda j, i: (0, j))],
        out_specs=pl.BlockSpec((tm, tn), lambda j, i: (i, j)),
        out_shape=jax.ShapeDtypeStruct((m, n), out_dtype),
        compiler_params=_params("parallel", "parallel"),
        name="matmul",
    )(a, w)


def _matmul_residual_kernel(a_ref, w_ref, r_ref, o_ref):
    o_ref[...] = r_ref[...] + jnp.dot(a_ref[...], w_ref[...], preferred_element_type=F32)


def _matmul_residual(a, w, r, tm=1024, tn=1024):
    m, k = a.shape
    _, n = w.shape
    return pl.pallas_call(
        _matmul_residual_kernel,
        grid=(n // tn, m // tm),
        in_specs=[pl.BlockSpec((tm, k), lambda j, i: (i, 0)),
                  pl.BlockSpec((k, tn), lambda j, i: (0, j)),
                  pl.BlockSpec((tm, tn), lambda j, i: (i, j))],
        out_specs=pl.BlockSpec((tm, tn), lambda j, i: (i, j)),
        out_shape=jax.ShapeDtypeStruct((m, n), F32),
        compiler_params=_params("parallel", "parallel"),
        name="matmul_residual",
    )(a, w, r)


def _conv_kernel(b_ref, c_ref, u_ref, w_ref, o_ref):
    z = c_ref[...] * u_ref[...]
    row = lax.broadcasted_iota(jnp.int32, z.shape, 0)
    w = w_ref[...]
    acc = z * w[2:3, :]
    for shift in (1, 2):
        zs = jnp.where(row >= shift, pltpu.roll(z, shift, axis=0), 0.0)
        acc = acc + zs * w[2 - shift:3 - shift, :]
    o_ref[...] = (b_ref[...] * acc).astype(o_ref.dtype)


def _short_conv(z, conv_w, batch, seq, tn=256):
    nb = CONV_DIM // tn
    return pl.pallas_call(
        _conv_kernel,
        grid=(batch, nb),
        in_specs=[pl.BlockSpec((seq, tn), lambda b, j: (b, COL_B // tn + j)),
                  pl.BlockSpec((seq, tn), lambda b, j: (b, COL_C // tn + j)),
                  pl.BlockSpec((seq, tn), lambda b, j: (b, COL_U // tn + j)),
                  pl.BlockSpec((CONV_WIDTH, tn), lambda b, j: (0, j))],
        out_specs=pl.BlockSpec((seq, tn), lambda b, j: (b, j)),
        out_shape=jax.ShapeDtypeStruct((batch * seq, CONV_DIM), BF16),
        compiler_params=_params("parallel", "parallel"),
        name="short_conv",
    )(z, z, z, conv_w)


def _moba_kernel(q_ref, k_ref, v_ref, o_ref, *, n_blk):
    seq = q_ref.shape[0]
    nt_dims = (((1,), (1,)), ((), ()))
    k = k_ref[...]
    kb = k.astype(BF16)
    vb = v_ref[...].astype(BF16)
    q_all = q_ref[...]
    k_mean = jnp.mean(k.reshape(n_blk, MOBA_BLOCK, HEAD_DIM), axis=1)

    gate = lax.dot_general(k_mean, q_all, nt_dims, precision=lax.Precision.HIGHEST,
                           preferred_element_type=F32)
    blk = lax.broadcasted_iota(jnp.int32, (n_blk, seq), 0)
    own = lax.broadcasted_iota(jnp.int32, (n_blk, seq), 1) // MOBA_BLOCK
    past = blk < own
    gate = jnp.where(past, gate, NEG_INF)
    rank = jnp.zeros((n_blk, seq), jnp.int32)
    for jp in range(n_blk - 1):
        gj = gate[jp:jp + 1, :]
        ahead = (gj > gate) | ((gj == gate) & (blk > jp))
        rank = rank + ahead.astype(jnp.int32)
    sel = jnp.where((rank < MOBA_TOPK) & past, 1.0, 0.0).astype(BF16)
    expand = jnp.where(lax.broadcasted_iota(jnp.int32, (n_blk, seq), 1) // MOBA_BLOCK == blk,
                       1.0, 0.0).astype(BF16)

    row = lax.broadcasted_iota(jnp.int32, (MOBA_BLOCK, MOBA_BLOCK), 0)
    col = lax.broadcasted_iota(jnp.int32, (MOBA_BLOCK, MOBA_BLOCK), 1)
    causal = col <= row
    q_scale = HEAD_DIM ** -0.5 * 1.4426950408889634
    for i in range(n_blk):
        lo, nk = i * MOBA_BLOCK, (i + 1) * MOBA_BLOCK
        q = (q_all[lo:nk] * q_scale).astype(BF16)
        s = lax.dot_general(q, kb[:nk], nt_dims, preferred_element_type=F32)
        s_own = jnp.where(causal, s[:, lo:nk], NEG_INF)
        if i > 0:
            keep = lax.dot_general(sel[:, lo:nk], expand[:, :lo], (((0,), (0,)), ((), ())),
                                   preferred_element_type=F32)
            s = jnp.concatenate([jnp.where(keep > 0.5, s[:, :lo], NEG_INF), s_own], axis=1)
        else:
            s = s_own
        m = jnp.max(s, axis=-1, keepdims=True)
        p = jnp.exp2(s - m)
        l = jnp.sum(p, axis=-1, keepdims=True)
        out = jnp.dot(p.astype(BF16), vb[:nk], preferred_element_type=F32) / l
        o_ref[lo:nk, :] = out.astype(o_ref.dtype)


def _moba_attention(z, batch, seq):
    n_blk = seq // MOBA_BLOCK
    spec = lambda off: pl.BlockSpec((seq, HEAD_DIM), lambda b, h: (b, off // HEAD_DIM + h))
    return pl.pallas_call(
        functools.partial(_moba_kernel, n_blk=n_blk),
        grid=(batch, N_HEADS),
        in_specs=[spec(COL_Q), spec(COL_K), spec(COL_V)],
        out_specs=pl.BlockSpec((seq, HEAD_DIM), lambda b, h: (b, h)),
        out_shape=jax.ShapeDtypeStruct((batch * seq, ATTN_DIM), BF16),
        compiler_params=_params("parallel", "parallel"),
        name="moba_attention",
    )(z, z, z)


def _merge_kernel(bz_ref, at_ref, wc_ref, wa_ref, gc_ref, ga_ref, bc_ref, ba_ref, o_ref):
    y_conv = jnp.dot(bz_ref[...], wc_ref[...], preferred_element_type=F32)
    y_attn = jnp.dot(at_ref[...], wa_ref[...], preferred_element_type=F32)
    merged = (jax.nn.sigmoid(gc_ref[...] + bc_ref[...]) * y_conv
              + jax.nn.sigmoid(ga_ref[...] + ba_ref[...]) * y_attn)
    o_ref[...] = merged.astype(o_ref.dtype)


def _merge(bz, attn, w_conv_out, w_attn_out, z, b_gate, tm=512, tn=1024):
    m = bz.shape[0]
    nb = D_MODEL // tn
    bias = b_gate.reshape(1, 2 * D_MODEL)
    return pl.pallas_call(
        _merge_kernel,
        grid=(nb, m // tm),
        in_specs=[pl.BlockSpec((tm, CONV_DIM), lambda j, i: (i, 0)),
                  pl.BlockSpec((tm, ATTN_DIM), lambda j, i: (i, 0)),
                  pl.BlockSpec((CONV_DIM, tn), lambda j, i: (0, j)),
                  pl.BlockSpec((ATTN_DIM, tn), lambda j, i: (0, j)),
                  pl.BlockSpec((tm, tn), lambda j, i: (i, COL_GC // tn + j)),
                  pl.BlockSpec((tm, tn), lambda j, i: (i, COL_GA // tn + j)),
                  pl.BlockSpec((1, tn), lambda j, i: (0, j)),
                  pl.BlockSpec((1, tn), lambda j, i: (0, nb + j))],
        out_specs=pl.BlockSpec((tm, tn), lambda j, i: (i, j)),
        out_shape=jax.ShapeDtypeStruct((m, D_MODEL), BF16),
        compiler_params=_params("parallel", "parallel"),
        name="merge",
    )(bz, attn, w_conv_out, w_attn_out, z, z, bias, bias)


SUBLANES = 8


def _sort16_pairs():
    pairs = []

    def merge(lo, n, r):
        step = r * 2
        if step < n:
            merge(lo, n, step)
            merge(lo + r, n, step)
            pairs.extend((i, i + r) for i in range(lo + r, lo + n - r, step))
        else:
            pairs.append((lo, lo + r))

    def sort(lo, n):
        if n > 1:
            sort(lo, n // 2)
            sort(lo + n // 2, n // 2)
            merge(lo, n, 1)

    sort(0, PEER_TOPK)
    return pairs


_SORT16 = _sort16_pairs()


def _compare_exchange(vals, i, j):
    a, b = vals[i], vals[j]
    if b is None:
        return
    if a is None:
        vals[i], vals[j] = b, None
        return
    vals[i], vals[j] = jnp.maximum(a, b), jnp.minimum(a, b)


def _top16(blocks):
    vals = list(blocks) + [None] * (PEER_TOPK - len(blocks))
    for i, j in _SORT16:
        _compare_exchange(vals, i, j)
    for shift in (4, 2, 1):
        other = [None if v is None else pltpu.roll(v, shift, axis=0) for v in vals]
        merged = []
        for i in range(PEER_TOPK):
            a, b = vals[i], other[PEER_TOPK - 1 - i]
            merged.append(b if a is None else a if b is None else jnp.maximum(a, b))
        vals = merged
        for step in (8, 4, 2, 1):
            for i in range(PEER_TOPK):
                if not i & step:
                    _compare_exchange(vals, i, i + step)
    return vals


def _sublane_sum(x):
    for shift in (4, 2, 1):
        x = x + pltpu.roll(x, shift, axis=0)
    return x


def _route_kernel(q_ref, sk_ref, n1_ref, rank2_ref, e1_ref, e2_ref):
    q = q_ref[...]
    tm = q.shape[0]
    dims = (((1,), (1,)), ((), ()))
    s1 = lax.dot_general(sk_ref[0, 0], q[:, :PEER_HALF], dims,
                         precision=lax.Precision.HIGHEST, preferred_element_type=F32)
    s2 = lax.dot_general(sk_ref[0, 1], q[:, PEER_HALF:], dims,
                         precision=lax.Precision.HIGHEST, preferred_element_type=F32)
    n_blocks = N_KEYS // SUBLANES
    s1_blocks = [s1[k * SUBLANES:(k + 1) * SUBLANES, :] for k in range(n_blocks)]
    s2_blocks = [s2[k * SUBLANES:(k + 1) * SUBLANES, :] for k in range(n_blocks)]
    t1 = _top16(s1_blocks)
    t2 = _top16(s2_blocks)

    sub = lax.broadcasted_iota(jnp.int32, (SUBLANES, tm), 0)
    spread = lambda rows: functools.reduce(
        lambda acc, b: jnp.where(sub == b, rows[b], acc), range(SUBLANES - 1), rows[SUBLANES - 1])
    t2_lo, t2_hi, t1_hi = spread(t2[:SUBLANES]), spread(t2[SUBLANES:]), spread(t1[SUBLANES:])
    cand = [t1[0] + t2_lo, t1[0] + t2_hi]
    for a in range(1, SUBLANES):
        n_valid = PEER_TOPK // (a + 1)
        piece = t1[a] + t2_lo
        cand.append(piece if n_valid >= SUBLANES else jnp.where(sub < n_valid, piece, NEG_INF))
    cand.append(t1_hi + t2[0])
    best = _top16(cand)
    tau = best[PEER_TOPK - 1]
    z = jnp.ones_like(tau)
    for r in range(1, PEER_TOPK):
        z = z + jnp.exp(best[r] - best[0])
    inv_z = 1.0 / z

    count = [_sublane_sum(jnp.where(t1[a] + t2_lo >= tau, 1.0, 0.0)
                          + jnp.where(t1[a] + t2_hi >= tau, 1.0, 0.0)) for a in range(PEER_TOPK)]
    n1, rank2, e1, e2 = [], [], [], []
    for k in range(n_blocks):
        n1_k = jnp.zeros((SUBLANES, tm), F32)
        rank2_k = jnp.full((SUBLANES, tm), float(PEER_TOPK), F32)
        for a in reversed(range(PEER_TOPK)):
            n1_k = jnp.where(s1_blocks[k] == t1[a], count[a], n1_k)
            rank2_k = jnp.where(s2_blocks[k] == t2[a], float(a), rank2_k)
        n1.append(n1_k)
        rank2.append(rank2_k)
        e1.append(jnp.exp(s1_blocks[k] - t1[0]) * inv_z)
        e2.append(jnp.exp(s2_blocks[k] - t2[0]))
    n1_ref[0] = jnp.concatenate(n1, axis=0)
    rank2_ref[0] = jnp.concatenate(rank2, axis=0).astype(rank2_ref.dtype)
    e1_ref[0] = jnp.concatenate(e1, axis=0)
    e2_ref[0] = jnp.concatenate(e2, axis=0).astype(e2_ref.dtype)


def _peer_route(qp, sub_keys, tm=512):
    t = qp.shape[0]
    key_out = pl.BlockSpec((1, N_KEYS, tm), lambda i, h: (h, 0, i))
    key_shape = lambda dtype: jax.ShapeDtypeStruct((PEER_HEADS, N_KEYS, t), dtype)
    return pl.pallas_call(
        _route_kernel,
        grid=(t // tm, PEER_HEADS),
        in_specs=[pl.BlockSpec((tm, 2 * PEER_HALF), lambda i, h: (i, h)),
                  pl.BlockSpec((1, 2, N_KEYS, PEER_HALF), lambda i, h: (h, 0, 0, 0))],
        out_specs=[key_out, key_out, key_out, key_out],
        out_shape=[key_shape(F32), key_shape(BF16), key_shape(F32), key_shape(BF16)],
        compiler_params=_params("parallel", "parallel"),
        name="peer_route",
    )(qp, sub_keys)


def _peer_kernel(xt_ref, u_ref, vt_ref, n1_ref, rank2_ref, e1_ref, e2_ref, o_ref, *, n_grp):
    c = pl.program_id(1)

    @pl.when(c == 0)
    def _():
        o_ref[...] = jnp.zeros_like(o_ref)

    hid = jnp.dot(u_ref[...], xt_ref[...], preferred_element_type=F32)
    sqrt_half = 0.7071067811865476
    w_parts = []
    for g in range(n_grp):
        i = c * n_grp + g
        gate = None
        for h in range(PEER_HEADS):
            n1 = n1_ref[h, pl.ds(i, 1), :].astype(BF16)
            e1 = e1_ref[h, pl.ds(i, 1), :].astype(BF16)
            contrib = jnp.where(rank2_ref[h] < n1, e1 * e2_ref[h], jnp.zeros((), BF16))
            gate = contrib if gate is None else gate + contrib
        hg = hid[g * N_KEYS:(g + 1) * N_KEYS, :]
        act = 0.5 * hg * (1.0 + lax.erf(hg * sqrt_half))
        w_parts.append(act.astype(BF16) * gate)
    w = w_parts[0] if n_grp == 1 else jnp.concatenate(w_parts, axis=0)
    o_ref[...] += jnp.dot(vt_ref[...], w, preferred_element_type=F32)


def _peer_mix(xnt, u, vt, n1, rank2, e1, e2, tt=512, n_grp=4):
    d, t = xnt.shape
    ec = n_grp * N_KEYS
    once = pl.Buffered(1)
    tok = pl.BlockSpec((PEER_HEADS, N_KEYS, tt), lambda i, c: (0, 0, i), pipeline_mode=once)
    return pl.pallas_call(
        functools.partial(_peer_kernel, n_grp=n_grp),
        grid=(t // tt, N_EXPERTS // ec),
        in_specs=[pl.BlockSpec((d, tt), lambda i, c: (0, i), pipeline_mode=once),
                  pl.BlockSpec((ec, d), lambda i, c: (c, 0)),
                  pl.BlockSpec((d, ec), lambda i, c: (0, c)),
                  tok, tok, tok, tok],
        out_specs=pl.BlockSpec((d, tt), lambda i, c: (0, i)),
        out_shape=jax.ShapeDtypeStruct((d, t), F32),
        compiler_params=_params("parallel", "arbitrary"),
        name="peer_mix",
    )(xnt, u, vt, n1, rank2, e1, e2)


def kernel(x, norm_mix, w_in, b_gate, conv_w, w_conv_out, w_attn_out, w_o, norm_ffn,
           w_peer_q, sub_keys, u_emb, v_emb, norm_final):
    batch, seq, d = x.shape
    assert w_in.shape[0] == 1, "single-layer block"
    h = x.reshape(batch * seq, d)
    hn = _rmsnorm(h, norm_mix[0], BF16)
    z = _matmul(hn, w_in[0].astype(BF16), F32)
    bz = _short_conv(z, conv_w[0], batch, seq)
    attn = _moba_attention(z, batch, seq)
    merged = _merge(bz, attn, w_conv_out[0].astype(BF16), w_attn_out[0].astype(BF16),
                    z, b_gate[0])
    h = _matmul_residual(merged, w_o[0].astype(BF16), h)
    hn, hnt = _rmsnorm_with_transpose(h, norm_ffn[0])
    qp = _matmul(hn, w_peer_q[0].astype(BF16), F32)
    n1, rank2, e1, e2 = _peer_route(qp, sub_keys[0])
    peer_t = _peer_mix(hnt, u_emb[0].astype(BF16), v_emb[0].T.astype(BF16), n1, rank2, e1, e2)
    return _add_rmsnorm(h, peer_t, norm_final).reshape(batch, seq, d)
```

```python
import functools

import jax
import jax.numpy as jnp
from jax import lax
from jax.experimental import pallas as pl
from jax.experimental.pallas import tpu as pltpu

D_MODEL = 4096
CONV_DIM = 2048
CONV_WIDTH = 3
N_HEADS = 16
HEAD_DIM = 128
ATTN_DIM = N_HEADS * HEAD_DIM
MOBA_BLOCK = 256
MOBA_TOPK = 3
PEER_HEADS = 8
N_KEYS = 128
N_EXPERTS = N_KEYS * N_KEYS
PEER_HALF = 128
PEER_TOPK = 16
EPS = 1e-6

COL_B = 0
COL_C = CONV_DIM
COL_U = 2 * CONV_DIM
COL_Q = 3 * CONV_DIM
COL_K = COL_Q + ATTN_DIM
COL_V = COL_K + ATTN_DIM
COL_GC = COL_V + ATTN_DIM
COL_GA = COL_GC + D_MODEL
IN_COLS = COL_GA + D_MODEL
Z_Q = 0
Z_K = COL_K - COL_Q
Z_V = COL_V - COL_Q
Z_GC = COL_GC - COL_Q
Z_GA = COL_GA - COL_Q

VMEM_LIMIT_BYTES = 56 * 1024 * 1024

F32 = jnp.float32
BF16 = jnp.bfloat16
NEG_INF = float("-inf")
SUBLANES = 8


def _params(*semantics):
    return pltpu.CompilerParams(dimension_semantics=semantics,
                                vmem_limit_bytes=VMEM_LIMIT_BYTES)


def _rmsnorm_kernel(x_ref, g_ref, o_ref):
    x = x_ref[...]
    y = x * lax.rsqrt(jnp.mean(x * x, axis=-1, keepdims=True) + EPS)
    o_ref[...] = (y * g_ref[...]).astype(o_ref.dtype)


def _rmsnorm(x, g, out_dtype, tm=256):
    m, d = x.shape
    return pl.pallas_call(
        _rmsnorm_kernel,
        grid=(m // tm,),
        in_specs=[pl.BlockSpec((tm, d), lambda i: (i, 0)),
                  pl.BlockSpec((1, d), lambda i: (0, 0))],
        out_specs=pl.BlockSpec((tm, d), lambda i: (i, 0)),
        out_shape=jax.ShapeDtypeStruct((m, d), out_dtype),
        compiler_params=_params("parallel"),
        name="rmsnorm",
    )(x, g.reshape(1, d))


def _rmsnorm_t_kernel(x_ref, g_ref, o_ref, ot_ref):
    x = x_ref[...]
    y = x * lax.rsqrt(jnp.mean(x * x, axis=-1, keepdims=True) + EPS) * g_ref[...]
    o_ref[...] = y.astype(o_ref.dtype)
    ot_ref[...] = y.T.astype(ot_ref.dtype)


def _rmsnorm_with_transpose(x, g, tm=256):
    m, d = x.shape
    return pl.pallas_call(
        _rmsnorm_t_kernel,
        grid=(m // tm,),
        in_specs=[pl.BlockSpec((tm, d), lambda i: (i, 0)),
                  pl.BlockSpec((1, d), lambda i: (0, 0))],
        out_specs=[pl.BlockSpec((tm, d), lambda i: (i, 0)),
                   pl.BlockSpec((d, tm), lambda i: (0, i))],
        out_shape=[jax.ShapeDtypeStruct((m, d), BF16),
                   jax.ShapeDtypeStruct((d, m), BF16)],
        compiler_params=_params("parallel"),
        name="rmsnorm_t",
    )(x, g.reshape(1, d))


def _add_rmsnorm_kernel(a_ref, bt_ref, g_ref, o_ref):
    x = a_ref[...] + bt_ref[...].T
    y = x * lax.rsqrt(jnp.mean(x * x, axis=-1, keepdims=True) + EPS)
    o_ref[...] = (y * g_ref[...]).astype(o_ref.dtype)


def _add_rmsnorm(a, bt, g, tm=256):
    m, d = a.shape
    return pl.pallas_call(
        _add_rmsnorm_kernel,
        grid=(m // tm,),
        in_specs=[pl.BlockSpec((tm, d), lambda i: (i, 0)),
                  pl.BlockSpec((d, tm), lambda i: (0, i)),
                  pl.BlockSpec((1, d), lambda i: (0, 0))],
        out_specs=pl.BlockSpec((tm, d), lambda i: (i, 0)),
        out_shape=jax.ShapeDtypeStruct((m, d), F32),
        compiler_params=_params("parallel"),
        name="add_rmsnorm",
    )(a, bt, g.reshape(1, d))


def _cast_at_first_row_tile(pairs):
    @pl.when(pl.program_id(1) == 0)
    def _():
        for src, dst in pairs:
            dst[...] = src[...].astype(dst.dtype)


def _matmul_kernel(a_ref, w_ref, o_ref, wb_ref):
    _cast_at_first_row_tile([(w_ref, wb_ref)])
    o_ref[...] = jnp.dot(a_ref[...], wb_ref[...],
                         preferred_element_type=F32).astype(o_ref.dtype)


def _matmul(a, w, out_dtype, col0, n, tm=512, tn=1024):
    m, k = a.shape
    return pl.pallas_call(
        _matmul_kernel,
        grid=(n // tn, m // tm),
        in_specs=[pl.BlockSpec((tm, k), lambda j, i: (i, 0)),
                  pl.BlockSpec((k, tn), lambda j, i: (0, col0 // tn + j))],
        out_specs=pl.BlockSpec((tm, tn), lambda j, i: (i, j)),
        out_shape=jax.ShapeDtypeStruct((m, n), out_dtype),
        scratch_shapes=[pltpu.VMEM((k, tn), BF16)],
        compiler_params=_params("parallel", "arbitrary"),
        name="matmul",
    )(a, w)


def _matmul_residual_kernel(a_ref, w_ref, r_ref, o_ref, wb_ref):
    _cast_at_first_row_tile([(w_ref, wb_ref)])
    o_ref[...] = r_ref[...] + jnp.dot(a_ref[...], wb_ref[...], preferred_element_type=F32)


def _matmul_residual(a, w, r, tm=1024, tn=512):
    m, k = a.shape
    _, n = w.shape
    return pl.pallas_call(
        _matmul_residual_kernel,
        grid=(n // tn, m // tm),
        in_specs=[pl.BlockSpec((tm, k), lambda j, i: (i, 0)),
                  pl.BlockSpec((k, tn), lambda j, i: (0, j)),
                  pl.BlockSpec((tm, tn), lambda j, i: (i, j))],
        out_specs=pl.BlockSpec((tm, tn), lambda j, i: (i, j)),
        out_shape=jax.ShapeDtypeStruct((m, n), F32),
        scratch_shapes=[pltpu.VMEM((k, tn), BF16)],
        compiler_params=_params("parallel", "arbitrary"),
        name="matmul_residual",
    )(a, w, r)


def _transpose_cast_kernel(x_ref, o_ref):
    o_ref[...] = x_ref[...].T.astype(o_ref.dtype)


def _transpose_cast(x, out_dtype, tr=512, tc=512):
    r, c = x.shape
    return pl.pallas_call(
        _transpose_cast_kernel,
        grid=(r // tr, c // tc),
        in_specs=[pl.BlockSpec((tr, tc), lambda i, j: (i, j))],
        out_specs=pl.BlockSpec((tc, tr), lambda i, j: (j, i)),
        out_shape=jax.ShapeDtypeStruct((c, r), out_dtype),
        compiler_params=_params("parallel", "parallel"),
        name="transpose_cast",
    )(x)


def _conv_proj_kernel(a_ref, wb_ref, wc_ref, wu_ref, cw_ref, o_ref,
                      wb_bf, wc_bf, wu_bf, hist_ref, *, tiles_per_seq):
    _cast_at_first_row_tile([(wb_ref, wb_bf), (wc_ref, wc_bf), (wu_ref, wu_bf)])

    @pl.when(pl.program_id(1) % tiles_per_seq == 0)
    def _():
        hist_ref[...] = jnp.zeros_like(hist_ref)

    a = a_ref[...]
    z = (jnp.dot(a, wc_bf[...], preferred_element_type=F32)
         * jnp.dot(a, wu_bf[...], preferred_element_type=F32))
    hist = hist_ref[...]
    n_hist = hist.shape[0]
    row = lax.broadcasted_iota(jnp.int32, hist.shape, 0)
    w = cw_ref[...]
    acc = z * w[2:3, :]
    for shift in (1, 2):
        zs = pltpu.roll(z, shift, axis=0)
        top = jnp.where(row >= shift, zs[:n_hist], pltpu.roll(hist, shift, axis=0))
        zs = jnp.concatenate([top, zs[n_hist:]], axis=0)
        acc = acc + zs * w[2 - shift:3 - shift, :]
    hist_ref[...] = z[z.shape[0] - n_hist:]
    o_ref[...] = (jnp.dot(a, wb_bf[...], preferred_element_type=F32) * acc).astype(o_ref.dtype)


def _conv_proj(a, w_in, conv_w, seq, tm=1024, tn=256):
    m, k = a.shape
    nb = CONV_DIM // tn
    wspec = lambda off: pl.BlockSpec((k, tn), lambda j, i: (0, off // tn + j))
    return pl.pallas_call(
        functools.partial(_conv_proj_kernel, tiles_per_seq=seq // tm),
        grid=(nb, m // tm),
        in_specs=[pl.BlockSpec((tm, k), lambda j, i: (i, 0)),
                  wspec(COL_B), wspec(COL_C), wspec(COL_U),
                  pl.BlockSpec((CONV_WIDTH, tn), lambda j, i: (0, j))],
        out_specs=pl.BlockSpec((tm, tn), lambda j, i: (i, j)),
        out_shape=jax.ShapeDtypeStruct((m, CONV_DIM), BF16),
        scratch_shapes=[pltpu.VMEM((k, tn), BF16)] * 3 + [pltpu.VMEM((SUBLANES, tn), F32)],
        compiler_params=_params("parallel", "arbitrary"),
        name="conv_proj",
    )(a, w_in, w_in, w_in, conv_w)


def _moba_kernel(q_ref, k_ref, v_ref, o_ref, *, n_blk):
    seq = q_ref.shape[0]
    nt_dims = (((1,), (1,)), ((), ()))
    k = k_ref[...]
    kb = k.astype(BF16)
    vb = v_ref[...].astype(BF16)
    q_all = q_ref[...]
    k_mean = jnp.mean(k.reshape(n_blk, MOBA_BLOCK, HEAD_DIM), axis=1)

    gate = lax.dot_general(k_mean, q_all, nt_dims, precision=lax.Precision.HIGHEST,
                           preferred_element_type=F32)
    blk = lax.broadcasted_iota(jnp.int32, (n_blk, seq), 0)
    own = lax.broadcasted_iota(jnp.int32, (n_blk, seq), 1) // MOBA_BLOCK
    past = blk < own
    gate = jnp.where(past, gate, NEG_INF)
    rank = jnp.zeros((n_blk, seq), jnp.int32)
    for jp in range(n_blk - 1):
        gj = gate[jp:jp + 1, :]
        ahead = (gj > gate) | ((gj == gate) & (blk > jp))
        rank = rank + ahead.astype(jnp.int32)
    sel = jnp.where((rank < MOBA_TOPK) & past, 1.0, 0.0).astype(BF16)
    expand = jnp.where(lax.broadcasted_iota(jnp.int32, (n_blk, seq), 1) // MOBA_BLOCK == blk,
                       1.0, 0.0).astype(BF16)

    row = lax.broadcasted_iota(jnp.int32, (MOBA_BLOCK, MOBA_BLOCK), 0)
    col = lax.broadcasted_iota(jnp.int32, (MOBA_BLOCK, MOBA_BLOCK), 1)
    causal = col <= row
    q_scale = HEAD_DIM ** -0.5 * 1.4426950408889634
    for i in range(n_blk):
        lo, nk = i * MOBA_BLOCK, (i + 1) * MOBA_BLOCK
        q = (q_all[lo:nk] * q_scale).astype(BF16)
        s = lax.dot_general(q, kb[:nk], nt_dims, preferred_element_type=F32)
        s_own = jnp.where(causal, s[:, lo:nk], NEG_INF)
        if i > 0:
            keep = lax.dot_general(sel[:, lo:nk], expand[:, :lo], (((0,), (0,)), ((), ())),
                                   preferred_element_type=F32)
            s = jnp.concatenate([jnp.where(keep > 0.5, s[:, :lo], NEG_INF), s_own], axis=1)
        else:
            s = s_own
        m = jnp.max(s, axis=-1, keepdims=True)
        p = jnp.exp2(s - m)
        l = jnp.sum(p, axis=-1, keepdims=True)
        out = jnp.dot(p.astype(BF16), vb[:nk], preferred_element_type=F32) / l
        o_ref[lo:nk, :] = out.astype(o_ref.dtype)


def _moba_attention(z, batch, seq):
    n_blk = seq // MOBA_BLOCK
    spec = lambda off: pl.BlockSpec((seq, HEAD_DIM), lambda b, h: (b, off // HEAD_DIM + h))
    return pl.pallas_call(
        functools.partial(_moba_kernel, n_blk=n_blk),
        grid=(batch, N_HEADS),
        in_specs=[spec(Z_Q), spec(Z_K), spec(Z_V)],
        out_specs=pl.BlockSpec((seq, HEAD_DIM), lambda b, h: (b, h)),
        out_shape=jax.ShapeDtypeStruct((batch * seq, ATTN_DIM), BF16),
        compiler_params=_params("parallel", "parallel"),
        name="moba_attention",
    )(z, z, z)


def _merge_kernel(bz_ref, at_ref, wc_ref, wa_ref, gc_ref, ga_ref, bc_ref, ba_ref, o_ref,
                  wc_bf, wa_bf):
    _cast_at_first_row_tile([(wc_ref, wc_bf), (wa_ref, wa_bf)])
    y_conv = jnp.dot(bz_ref[...], wc_bf[...], preferred_element_type=F32)
    y_attn = jnp.dot(at_ref[...], wa_bf[...], preferred_element_type=F32)
    merged = (jax.nn.sigmoid(gc_ref[...] + bc_ref[...]) * y_conv
              + jax.nn.sigmoid(ga_ref[...] + ba_ref[...]) * y_attn)
    o_ref[...] = merged.astype(o_ref.dtype)


def _merge(bz, attn, w_conv_out, w_attn_out, z, b_gate, tm=512, tn=512):
    m = bz.shape[0]
    nb = D_MODEL // tn
    bias = b_gate.reshape(1, 2 * D_MODEL)
    return pl.pallas_call(
        _merge_kernel,
        grid=(nb, m // tm),
        in_specs=[pl.BlockSpec((tm, CONV_DIM), lambda j, i: (i, 0)),
                  pl.BlockSpec((tm, ATTN_DIM), lambda j, i: (i, 0)),
                  pl.BlockSpec((CONV_DIM, tn), lambda j, i: (0, j)),
                  pl.BlockSpec((ATTN_DIM, tn), lambda j, i: (0, j)),
                  pl.BlockSpec((tm, tn), lambda j, i: (i, Z_GC // tn + j)),
                  pl.BlockSpec((tm, tn), lambda j, i: (i, Z_GA // tn + j)),
                  pl.BlockSpec((1, tn), lambda j, i: (0, j)),
                  pl.BlockSpec((1, tn), lambda j, i: (0, nb + j))],
        out_specs=pl.BlockSpec((tm, tn), lambda j, i: (i, j)),
        out_shape=jax.ShapeDtypeStruct((m, D_MODEL), BF16),
        scratch_shapes=[pltpu.VMEM((CONV_DIM, tn), BF16), pltpu.VMEM((ATTN_DIM, tn), BF16)],
        compiler_params=_params("parallel", "arbitrary"),
        name="merge",
    )(bz, attn, w_conv_out, w_attn_out, z, z, bias, bias)


def _sort16_pairs():
    pairs = []

    def merge(lo, n, r):
        step = r * 2
        if step < n:
            merge(lo, n, step)
            merge(lo + r, n, step)
            pairs.extend((i, i + r) for i in range(lo + r, lo + n - r, step))
        else:
            pairs.append((lo, lo + r))

    def sort(lo, n):
        if n > 1:
            sort(lo, n // 2)
            sort(lo + n // 2, n // 2)
            merge(lo, n, 1)

    sort(0, PEER_TOPK)
    return pairs


_SORT16 = _sort16_pairs()


def _compare_exchange(vals, i, j):
    a, b = vals[i], vals[j]
    if b is None:
        return
    if a is None:
        vals[i], vals[j] = b, None
        return
    vals[i], vals[j] = jnp.maximum(a, b), jnp.minimum(a, b)


def _top16(blocks):
    vals = list(blocks) + [None] * (PEER_TOPK - len(blocks))
    for i, j in _SORT16:
        _compare_exchange(vals, i, j)
    for shift in (4, 2, 1):
        other = [None if v is None else pltpu.roll(v, shift, axis=0) for v in vals]
        merged = []
        for i in range(PEER_TOPK):
            a, b = vals[i], other[PEER_TOPK - 1 - i]
            merged.append(b if a is None else a if b is None else jnp.maximum(a, b))
        vals = merged
        for step in (8, 4, 2, 1):
            for i in range(PEER_TOPK):
                if not i & step:
                    _compare_exchange(vals, i, i + step)
    return vals


def _sublane_sum(x):
    for shift in (4, 2, 1):
        x = x + pltpu.roll(x, shift, axis=0)
    return x


def _route_kernel(q_ref, sk_ref, n1_ref, rank2_ref, e1_ref, e2_ref):
    q = q_ref[...]
    tm = q.shape[0]
    dims = (((1,), (1,)), ((), ()))
    s1 = lax.dot_general(sk_ref[0, 0], q[:, :PEER_HALF], dims,
                         precision=lax.Precision.HIGHEST, preferred_element_type=F32)
    s2 = lax.dot_general(sk_ref[0, 1], q[:, PEER_HALF:], dims,
                         precision=lax.Precision.HIGHEST, preferred_element_type=F32)
    n_blocks = N_KEYS // SUBLANES
    s1_blocks = [s1[k * SUBLANES:(k + 1) * SUBLANES, :] for k in range(n_blocks)]
    s2_blocks = [s2[k * SUBLANES:(k + 1) * SUBLANES, :] for k in range(n_blocks)]
    t1 = _top16(s1_blocks)
    t2 = _top16(s2_blocks)

    sub = lax.broadcasted_iota(jnp.int32, (SUBLANES, tm), 0)
    spread = lambda rows: functools.reduce(
        lambda acc, b: jnp.where(sub == b, rows[b], acc), range(SUBLANES - 1), rows[SUBLANES - 1])
    t2_lo, t2_hi, t1_hi = spread(t2[:SUBLANES]), spread(t2[SUBLANES:]), spread(t1[SUBLANES:])
    cand = [t1[0] + t2_lo, t1[0] + t2_hi]
    for a in range(1, SUBLANES):
        n_valid = PEER_TOPK // (a + 1)
        piece = t1[a] + t2_lo
        cand.append(piece if n_valid >= SUBLANES else jnp.where(sub < n_valid, piece, NEG_INF))
    cand.append(t1_hi + t2[0])
    best = _top16(cand)
    tau = best[PEER_TOPK - 1]
    z = jnp.ones_like(tau)
    for r in range(1, PEER_TOPK):
        z = z + jnp.exp(best[r] - best[0])
    inv_z = 1.0 / z

    count = [_sublane_sum(jnp.where(t1[a] + t2_lo >= tau, 1.0, 0.0)
                          + jnp.where(t1[a] + t2_hi >= tau, 1.0, 0.0)) for a in range(PEER_TOPK)]
    n1, rank2, e1, e2 = [], [], [], []
    for k in range(n_blocks):
        n1_k = jnp.zeros((SUBLANES, tm), F32)
        rank2_k = jnp.full((SUBLANES, tm), float(PEER_TOPK), F32)
        for a in reversed(range(PEER_TOPK)):
            n1_k = jnp.where(s1_blocks[k] == t1[a], count[a], n1_k)
            rank2_k = jnp.where(s2_blocks[k] == t2[a], float(a), rank2_k)
        n1.append(n1_k)
        rank2.append(rank2_k)
        e1.append(jnp.exp(s1_blocks[k] - t1[0]) * inv_z)
        e2.append(jnp.exp(s2_blocks[k] - t2[0]))
    n1_ref[0] = jnp.concatenate(n1, axis=0)
    rank2_ref[0] = jnp.concatenate(rank2, axis=0).astype(rank2_ref.dtype)
    e1_ref[0] = jnp.concatenate(e1, axis=0)
    e2_ref[0] = jnp.concatenate(e2, axis=0).astype(e2_ref.dtype)


def _peer_route(qp, sub_keys, tm=512):
    t = qp.shape[0]
    key_out = pl.BlockSpec((1, N_KEYS, tm), lambda i, h: (h, 0, i))
    key_shape = lambda dtype: jax.ShapeDtypeStruct((PEER_HEADS, N_KEYS, t), dtype)
    return pl.pallas_call(
        _route_kernel,
        grid=(t // tm, PEER_HEADS),
        in_specs=[pl.BlockSpec((tm, 2 * PEER_HALF), lambda i, h: (i, h)),
                  pl.BlockSpec((1, 2, N_KEYS, PEER_HALF), lambda i, h: (h, 0, 0, 0))],
        out_specs=[key_out, key_out, key_out, key_out],
        out_shape=[key_shape(F32), key_shape(BF16), key_shape(F32), key_shape(BF16)],
        compiler_params=_params("parallel", "parallel"),
        name="peer_route",
    )(qp, sub_keys)


def _peer_kernel(xt_ref, u_ref, vt_ref, n1_ref, rank2_ref, e1_ref, e2_ref, o_ref, *, n_grp):
    c = pl.program_id(1)

    @pl.when(c == 0)
    def _():
        o_ref[...] = jnp.zeros_like(o_ref)

    hid = jnp.dot(u_ref[...], xt_ref[...], preferred_element_type=F32)
    sqrt_half = 0.7071067811865476
    w_parts = []
    for g in range(n_grp):
        i = c * n_grp + g
        gate = None
        for h in range(PEER_HEADS):
            n1 = n1_ref[h, pl.ds(i, 1), :].astype(BF16)
            e1 = e1_ref[h, pl.ds(i, 1), :].astype(BF16)
            contrib = jnp.where(rank2_ref[h] < n1, e1 * e2_ref[h], jnp.zeros((), BF16))
            gate = contrib if gate is None else gate + contrib
        hg = hid[g * N_KEYS:(g + 1) * N_KEYS, :]
        act = 0.5 * hg * (1.0 + lax.erf(hg * sqrt_half))
        w_parts.append(act.astype(BF16) * gate)
    w = w_parts[0] if n_grp == 1 else jnp.concatenate(w_parts, axis=0)
    o_ref[...] += jnp.dot(vt_ref[...], w, preferred_element_type=F32)


def _peer_mix(xnt, u, vt, n1, rank2, e1, e2, tt=512, n_grp=4):
    d, t = xnt.shape
    ec = n_grp * N_KEYS
    once = pl.Buffered(1)
    tok = pl.BlockSpec((PEER_HEADS, N_KEYS, tt), lambda i, c: (0, 0, i), pipeline_mode=once)
    return pl.pallas_call(
        functools.partial(_peer_kernel, n_grp=n_grp),
        grid=(t // tt, N_EXPERTS // ec),
        in_specs=[pl.BlockSpec((d, tt), lambda i, c: (0, i), pipeline_mode=once),
                  pl.BlockSpec((ec, d), lambda i, c: (c, 0)),
                  pl.BlockSpec((d, ec), lambda i, c: (0, c)),
                  tok, tok, tok, tok],
        out_specs=pl.BlockSpec((d, tt), lambda i, c: (0, i)),
        out_shape=jax.ShapeDtypeStruct((d, t), F32),
        compiler_params=_params("parallel", "arbitrary"),
        name="peer_mix",
    )(xnt, u, vt, n1, rank2, e1, e2)


def kernel(x, norm_mix, w_in, b_gate, conv_w, w_conv_out, w_attn_out, w_o, norm_ffn,
           w_peer_q, sub_keys, u_emb, v_emb, norm_final):
    batch, seq, d = x.shape
    assert w_in.shape[0] == 1, "single-layer block"
    h = x.reshape(batch * seq, d)
    hn = _rmsnorm(h, norm_mix[0], BF16)
    bz = _conv_proj(hn, w_in[0], conv_w[0], seq)
    z = _matmul(hn, w_in[0], F32, col0=COL_Q, n=IN_COLS - COL_Q)
    attn = _moba_attention(z, batch, seq)
    merged = _merge(bz, attn, w_conv_out[0], w_attn_out[0], z, b_gate[0])
    h = _matmul_residual(merged, w_o[0], h)
    hn, hnt = _rmsnorm_with_transpose(h, norm_ffn[0])
    qp = _matmul(hn, w_peer_q[0], F32, col0=0, n=w_peer_q.shape[-1])
    n1, rank2, e1, e2 = _peer_route(qp, sub_keys[0])
    peer_t = _peer_mix(hnt, u_emb[0].astype(BF16), _transpose_cast(v_emb[0], BF16),
                       n1, rank2, e1, e2)
    return _add_rmsnorm(h, peer_t, norm_final).reshape(batch, seq, d)
```

```python
import functools

import jax
import jax.numpy as jnp
from jax import lax
from jax.experimental import pallas as pl
from jax.experimental.pallas import tpu as pltpu

D_MODEL = 4096
CONV_DIM = 2048
CONV_WIDTH = 3
N_HEADS = 16
HEAD_DIM = 128
ATTN_DIM = N_HEADS * HEAD_DIM
MOBA_BLOCK = 256
MOBA_TOPK = 3
PEER_HEADS = 8
N_KEYS = 128
N_EXPERTS = N_KEYS * N_KEYS
PEER_HALF = 128
PEER_TOPK = 16
EPS = 1e-6

COL_B = 0
COL_C = CONV_DIM
COL_U = 2 * CONV_DIM
COL_Q = 3 * CONV_DIM
COL_K = COL_Q + ATTN_DIM
COL_V = COL_K + ATTN_DIM
COL_GC = COL_V + ATTN_DIM
COL_GA = COL_GC + D_MODEL
IN_COLS = COL_GA + D_MODEL
Z_Q = 0
Z_K = COL_K - COL_Q
Z_V = COL_V - COL_Q
Z_GC = COL_GC - COL_Q
Z_GA = COL_GA - COL_Q

VMEM_LIMIT_BYTES = 56 * 1024 * 1024

F32 = jnp.float32
BF16 = jnp.bfloat16
NEG_INF = float("-inf")
SUBLANES = 8


def _params(*semantics):
    return pltpu.CompilerParams(dimension_semantics=semantics,
                                vmem_limit_bytes=VMEM_LIMIT_BYTES)


def _rmsnorm_kernel(x_ref, g_ref, o_ref):
    x = x_ref[...]
    y = x * lax.rsqrt(jnp.mean(x * x, axis=-1, keepdims=True) + EPS)
    o_ref[...] = (y * g_ref[...]).astype(o_ref.dtype)


def _rmsnorm(x, g, out_dtype, tm=256):
    m, d = x.shape
    return pl.pallas_call(
        _rmsnorm_kernel,
        grid=(m // tm,),
        in_specs=[pl.BlockSpec((tm, d), lambda i: (i, 0)),
                  pl.BlockSpec((1, d), lambda i: (0, 0))],
        out_specs=pl.BlockSpec((tm, d), lambda i: (i, 0)),
        out_shape=jax.ShapeDtypeStruct((m, d), out_dtype),
        compiler_params=_params("parallel"),
        name="rmsnorm",
    )(x, g.reshape(1, d))


def _rmsnorm_t_kernel(x_ref, g_ref, o_ref, ot_ref):
    x = x_ref[...]
    y = x * lax.rsqrt(jnp.mean(x * x, axis=-1, keepdims=True) + EPS) * g_ref[...]
    o_ref[...] = y.astype(o_ref.dtype)
    ot_ref[...] = y.T.astype(ot_ref.dtype)


def _rmsnorm_with_transpose(x, g, tm=256):
    m, d = x.shape
    return pl.pallas_call(
        _rmsnorm_t_kernel,
        grid=(m // tm,),
        in_specs=[pl.BlockSpec((tm, d), lambda i: (i, 0)),
                  pl.BlockSpec((1, d), lambda i: (0, 0))],
        out_specs=[pl.BlockSpec((tm, d), lambda i: (i, 0)),
                   pl.BlockSpec((d, tm), lambda i: (0, i))],
        out_shape=[jax.ShapeDtypeStruct((m, d), BF16),
                   jax.ShapeDtypeStruct((d, m), BF16)],
        compiler_params=_params("parallel"),
        name="rmsnorm_t",
    )(x, g.reshape(1, d))


def _add_rmsnorm_kernel(a_ref, bt_ref, g_ref, o_ref):
    x = a_ref[...] + bt_ref[...].T
    y = x * lax.rsqrt(jnp.mean(x * x, axis=-1, keepdims=True) + EPS)
    o_ref[...] = (y * g_ref[...]).astype(o_ref.dtype)


def _add_rmsnorm(a, bt, g, tm=256):
    m, d = a.shape
    return pl.pallas_call(
        _add_rmsnorm_kernel,
        grid=(m // tm,),
        in_specs=[pl.BlockSpec((tm, d), lambda i: (i, 0)),
                  pl.BlockSpec((d, tm), lambda i: (0, i)),
                  pl.BlockSpec((1, d), lambda i: (0, 0))],
        out_specs=pl.BlockSpec((tm, d), lambda i: (i, 0)),
        out_shape=jax.ShapeDtypeStruct((m, d), F32),
        compiler_params=_params("parallel"),
        name="add_rmsnorm",
    )(a, bt, g.reshape(1, d))


def _cast_at_first_row_tile(pairs):
    @pl.when(pl.program_id(1) == 0)
    def _():
        for src, dst in pairs:
            dst[...] = src[...].astype(dst.dtype)


def _matmul_kernel(a_ref, w_ref, o_ref, wb_ref):
    _cast_at_first_row_tile([(w_ref, wb_ref)])
    o_ref[...] = jnp.dot(a_ref[...], wb_ref[...],
                         preferred_element_type=F32).astype(o_ref.dtype)


def _matmul(a, w, out_dtype, col0, n, tm=512, tn=1024):
    m, k = a.shape
    return pl.pallas_call(
        _matmul_kernel,
        grid=(n // tn, m // tm),
        in_specs=[pl.BlockSpec((tm, k), lambda j, i: (i, 0)),
                  pl.BlockSpec((k, tn), lambda j, i: (0, col0 // tn + j))],
        out_specs=pl.BlockSpec((tm, tn), lambda j, i: (i, j)),
        out_shape=jax.ShapeDtypeStruct((m, n), out_dtype),
        scratch_shapes=[pltpu.VMEM((k, tn), BF16)],
        compiler_params=_params("parallel", "arbitrary"),
        name="matmul",
    )(a, w)


def _matmul_residual_kernel(a_ref, w_ref, r_ref, o_ref):
    o_ref[...] = r_ref[...] + jnp.dot(a_ref[...], w_ref[...], preferred_element_type=F32)


def _matmul_residual(a, w, r, tm=1024, tn=1024):
    m, k = a.shape
    _, n = w.shape
    return pl.pallas_call(
        _matmul_residual_kernel,
        grid=(n // tn, m // tm),
        in_specs=[pl.BlockSpec((tm, k), lambda j, i: (i, 0)),
                  pl.BlockSpec((k, tn), lambda j, i: (0, j)),
                  pl.BlockSpec((tm, tn), lambda j, i: (i, j))],
        out_specs=pl.BlockSpec((tm, tn), lambda j, i: (i, j)),
        out_shape=jax.ShapeDtypeStruct((m, n), F32),
        compiler_params=_params("parallel", "parallel"),
        name="matmul_residual",
    )(a, w, r)


def _conv_proj_kernel(a_ref, wb_ref, wc_ref, wu_ref, cw_ref, o_ref,
                      wb_bf, wc_bf, wu_bf, hist_ref, *, tiles_per_seq):
    _cast_at_first_row_tile([(wb_ref, wb_bf), (wc_ref, wc_bf), (wu_ref, wu_bf)])

    @pl.when(pl.program_id(1) % tiles_per_seq == 0)
    def _():
        hist_ref[...] = jnp.zeros_like(hist_ref)

    a = a_ref[...]
    z = (jnp.dot(a, wc_bf[...], preferred_element_type=F32)
         * jnp.dot(a, wu_bf[...], preferred_element_type=F32))
    hist = hist_ref[...]
    n_hist = hist.shape[0]
    row = lax.broadcasted_iota(jnp.int32, hist.shape, 0)
    w = cw_ref[...]
    acc = z * w[2:3, :]
    for shift in (1, 2):
        zs = pltpu.roll(z, shift, axis=0)
        top = jnp.where(row >= shift, zs[:n_hist], pltpu.roll(hist, shift, axis=0))
        zs = jnp.concatenate([top, zs[n_hist:]], axis=0)
        acc = acc + zs * w[2 - shift:3 - shift, :]
    hist_ref[...] = z[z.shape[0] - n_hist:]
    o_ref[...] = (jnp.dot(a, wb_bf[...], preferred_element_type=F32) * acc).astype(o_ref.dtype)


def _conv_proj(a, w_in, conv_w, seq, tm=1024, tn=256):
    m, k = a.shape
    nb = CONV_DIM // tn
    wspec = lambda off: pl.BlockSpec((k, tn), lambda j, i: (0, off // tn + j))
    return pl.pallas_call(
        functools.partial(_conv_proj_kernel, tiles_per_seq=seq // tm),
        grid=(nb, m // tm),
        in_specs=[pl.BlockSpec((tm, k), lambda j, i: (i, 0)),
                  wspec(COL_B), wspec(COL_C), wspec(COL_U),
                  pl.BlockSpec((CONV_WIDTH, tn), lambda j, i: (0, j))],
        out_specs=pl.BlockSpec((tm, tn), lambda j, i: (i, j)),
        out_shape=jax.ShapeDtypeStruct((m, CONV_DIM), BF16),
        scratch_shapes=[pltpu.VMEM((k, tn), BF16)] * 3 + [pltpu.VMEM((SUBLANES, tn), F32)],
        compiler_params=_params("parallel", "arbitrary"),
        name="conv_proj",
    )(a, w_in, w_in, w_in, conv_w)


def _moba_kernel(q_ref, k_ref, v_ref, tab_ref, o_ref, tab_t_ref, *, n_blk):
    tab_t_ref[...] = tab_ref[...].T.astype(tab_t_ref.dtype)
    seq = q_ref.shape[0]
    nt_dims = (((1,), (1,)), ((), ()))
    k = k_ref[...]
    kb = k.astype(BF16)
    vb = v_ref[...].astype(BF16)
    q_all = q_ref[...]
    k_mean = jnp.mean(k.reshape(n_blk, MOBA_BLOCK, HEAD_DIM), axis=1)

    gate = lax.dot_general(k_mean, q_all, nt_dims, precision=lax.Precision.HIGHEST,
                           preferred_element_type=F32)
    blk = lax.broadcasted_iota(jnp.int32, (n_blk, seq), 0)
    own = lax.broadcasted_iota(jnp.int32, (n_blk, seq), 1) // MOBA_BLOCK
    past = blk < own
    gate = jnp.where(past, gate, NEG_INF)
    rank = jnp.zeros((n_blk, seq), jnp.int32)
    for jp in range(n_blk - 1):
        gj = gate[jp:jp + 1, :]
        ahead = (gj > gate) | ((gj == gate) & (blk > jp))
        rank = rank + ahead.astype(jnp.int32)
    sel = jnp.where((rank < MOBA_TOPK) & past, 1.0, 0.0).astype(BF16)
    expand = jnp.where(lax.broadcasted_iota(jnp.int32, (n_blk, seq), 1) // MOBA_BLOCK == blk,
                       1.0, 0.0).astype(BF16)

    row = lax.broadcasted_iota(jnp.int32, (MOBA_BLOCK, MOBA_BLOCK), 0)
    col = lax.broadcasted_iota(jnp.int32, (MOBA_BLOCK, MOBA_BLOCK), 1)
    causal = col <= row
    q_scale = HEAD_DIM ** -0.5 * 1.4426950408889634
    for i in range(n_blk):
        lo, nk = i * MOBA_BLOCK, (i + 1) * MOBA_BLOCK
        q = (q_all[lo:nk] * q_scale).astype(BF16)
        s = lax.dot_general(q, kb[:nk], nt_dims, preferred_element_type=F32)
        s_own = jnp.where(causal, s[:, lo:nk], NEG_INF)
        if i > 0:
            keep = lax.dot_general(sel[:, lo:nk], expand[:, :lo], (((0,), (0,)), ((), ())),
                                   preferred_element_type=F32)
            s = jnp.concatenate([jnp.where(keep > 0.5, s[:, :lo], NEG_INF), s_own], axis=1)
        else:
            s = s_own
        m = jnp.max(s, axis=-1, keepdims=True)
        p = jnp.exp2(s - m)
        l = jnp.sum(p, axis=-1, keepdims=True)
        out = jnp.dot(p.astype(BF16), vb[:nk], preferred_element_type=F32) / l
        o_ref[lo:nk, :] = out.astype(o_ref.dtype)


def _moba_attention(z, table, batch, seq):
    n_blk = seq // MOBA_BLOCK
    rows, cols = table.shape
    slab = rows // (batch * N_HEADS)
    spec = lambda off: pl.BlockSpec((seq, HEAD_DIM), lambda b, h: (b, off // HEAD_DIM + h))
    return pl.pallas_call(
        functools.partial(_moba_kernel, n_blk=n_blk),
        grid=(batch, N_HEADS),
        in_specs=[spec(Z_Q), spec(Z_K), spec(Z_V),
                  pl.BlockSpec((slab, cols), lambda b, h: (b * N_HEADS + h, 0))],
        out_specs=[pl.BlockSpec((seq, HEAD_DIM), lambda b, h: (b, h)),
                   pl.BlockSpec((cols, slab), lambda b, h: (0, b * N_HEADS + h))],
        out_shape=[jax.ShapeDtypeStruct((batch * seq, ATTN_DIM), BF16),
                   jax.ShapeDtypeStruct((cols, rows), BF16)],
        compiler_params=_params("parallel", "parallel"),
        name="moba_attention",
    )(z, z, z, table)


def _merge_kernel(bz_ref, at_ref, wc_ref, wa_ref, gc_ref, ga_ref, bc_ref, ba_ref, tab_ref,
                  o_ref, tab_bf_ref, wc_bf, wa_bf):
    tab_bf_ref[...] = tab_ref[...].astype(tab_bf_ref.dtype)
    _cast_at_first_row_tile([(wc_ref, wc_bf), (wa_ref, wa_bf)])
    y_conv = jnp.dot(bz_ref[...], wc_bf[...], preferred_element_type=F32)
    y_attn = jnp.dot(at_ref[...], wa_bf[...], preferred_element_type=F32)
    merged = (jax.nn.sigmoid(gc_ref[...] + bc_ref[...]) * y_conv
              + jax.nn.sigmoid(ga_ref[...] + ba_ref[...]) * y_attn)
    o_ref[...] = merged.astype(o_ref.dtype)


def _merge(bz, attn, w_conv_out, w_attn_out, z, b_gate, table, tm=512, tn=512):
    m = bz.shape[0]
    nb = D_MODEL // tn
    n_row_tiles = m // tm
    rows, cols = table.shape
    slab = rows // (nb * n_row_tiles)
    bias = b_gate.reshape(1, 2 * D_MODEL)
    return pl.pallas_call(
        _merge_kernel,
        grid=(nb, n_row_tiles),
        in_specs=[pl.BlockSpec((tm, CONV_DIM), lambda j, i: (i, 0)),
                  pl.BlockSpec((tm, ATTN_DIM), lambda j, i: (i, 0)),
                  pl.BlockSpec((CONV_DIM, tn), lambda j, i: (0, j)),
                  pl.BlockSpec((ATTN_DIM, tn), lambda j, i: (0, j)),
                  pl.BlockSpec((tm, tn), lambda j, i: (i, Z_GC // tn + j)),
                  pl.BlockSpec((tm, tn), lambda j, i: (i, Z_GA // tn + j)),
                  pl.BlockSpec((1, tn), lambda j, i: (0, j)),
                  pl.BlockSpec((1, tn), lambda j, i: (0, nb + j)),
                  pl.BlockSpec((slab, cols), lambda j, i: (j * n_row_tiles + i, 0))],
        out_specs=[pl.BlockSpec((tm, tn), lambda j, i: (i, j)),
                   pl.BlockSpec((slab, cols), lambda j, i: (j * n_row_tiles + i, 0))],
        out_shape=[jax.ShapeDtypeStruct((m, D_MODEL), BF16),
                   jax.ShapeDtypeStruct((rows, cols), BF16)],
        scratch_shapes=[pltpu.VMEM((CONV_DIM, tn), BF16), pltpu.VMEM((ATTN_DIM, tn), BF16)],
        compiler_params=_params("parallel", "arbitrary"),
        name="merge",
    )(bz, attn, w_conv_out, w_attn_out, z, z, bias, bias, table)


def _sort16_pairs():
    pairs = []

    def merge(lo, n, r):
        step = r * 2
        if step < n:
            merge(lo, n, step)
            merge(lo + r, n, step)
            pairs.extend((i, i + r) for i in range(lo + r, lo + n - r, step))
        else:
            pairs.append((lo, lo + r))

    def sort(lo, n):
        if n > 1:
            sort(lo, n // 2)
            sort(lo + n // 2, n // 2)
            merge(lo, n, 1)

    sort(0, PEER_TOPK)
    return pairs


_SORT16 = _sort16_pairs()


def _compare_exchange(vals, i, j):
    a, b = vals[i], vals[j]
    if b is None:
        return
    if a is None:
        vals[i], vals[j] = b, None
        return
    vals[i], vals[j] = jnp.maximum(a, b), jnp.minimum(a, b)


def _top16(blocks):
    vals = list(blocks) + [None] * (PEER_TOPK - len(blocks))
    for i, j in _SORT16:
        _compare_exchange(vals, i, j)
    for shift in (4, 2, 1):
        other = [None if v is None else pltpu.roll(v, shift, axis=0) for v in vals]
        merged = []
        for i in range(PEER_TOPK):
            a, b = vals[i], other[PEER_TOPK - 1 - i]
            merged.append(b if a is None else a if b is None else jnp.maximum(a, b))
        vals = merged
        for step in (8, 4, 2, 1):
            for i in range(PEER_TOPK):
                if not i & step:
                    _compare_exchange(vals, i, i + step)
    return vals


def _sublane_sum(x):
    for shift in (4, 2, 1):
        x = x + pltpu.roll(x, shift, axis=0)
    return x


def _route_kernel(q_ref, sk_ref, n1_ref, rank2_ref, e1_ref, e2_ref):
    q = q_ref[...]
    tm = q.shape[0]
    dims = (((1,), (1,)), ((), ()))
    s1 = lax.dot_general(sk_ref[0, 0], q[:, :PEER_HALF], dims,
                         precision=lax.Precision.HIGHEST, preferred_element_type=F32)
    s2 = lax.dot_general(sk_ref[0, 1], q[:, PEER_HALF:], dims,
                         precision=lax.Precision.HIGHEST, preferred_element_type=F32)
    n_blocks = N_KEYS // SUBLANES
    s1_blocks = [s1[k * SUBLANES:(k + 1) * SUBLANES, :] for k in range(n_blocks)]
    s2_blocks = [s2[k * SUBLANES:(k + 1) * SUBLANES, :] for k in range(n_blocks)]
    t1 = _top16(s1_blocks)
    t2 = _top16(s2_blocks)

    sub = lax.broadcasted_iota(jnp.int32, (SUBLANES, tm), 0)
    spread = lambda rows: functools.reduce(
        lambda acc, b: jnp.where(sub == b, rows[b], acc), range(SUBLANES - 1), rows[SUBLANES - 1])
    t2_lo, t2_hi, t1_hi = spread(t2[:SUBLANES]), spread(t2[SUBLANES:]), spread(t1[SUBLANES:])
    cand = [t1[0] + t2_lo, t1[0] + t2_hi]
    for a in range(1, SUBLANES):
        n_valid = PEER_TOPK // (a + 1)
        piece = t1[a] + t2_lo
        cand.append(piece if n_valid >= SUBLANES else jnp.where(sub < n_valid, piece, NEG_INF))
    cand.append(t1_hi + t2[0])
    best = _top16(cand)
    tau = best[PEER_TOPK - 1]
    z = jnp.ones_like(tau)
    for r in range(1, PEER_TOPK):
        z = z + jnp.exp(best[r] - best[0])
    inv_z = 1.0 / z

    count = [_sublane_sum(jnp.where(t1[a] + t2_lo >= tau, 1.0, 0.0)
                          + jnp.where(t1[a] + t2_hi >= tau, 1.0, 0.0)) for a in range(PEER_TOPK)]
    n1, rank2, e1, e2 = [], [], [], []
    for k in range(n_blocks):
        n1_k = jnp.zeros((SUBLANES, tm), F32)
        rank2_k = jnp.full((SUBLANES, tm), float(PEER_TOPK), F32)
        for a in reversed(range(PEER_TOPK)):
            n1_k = jnp.where(s1_blocks[k] == t1[a], count[a], n1_k)
            rank2_k = jnp.where(s2_blocks[k] == t2[a], float(a), rank2_k)
        n1.append(n1_k)
        rank2.append(rank2_k)
        e1.append(jnp.exp(s1_blocks[k] - t1[0]) * inv_z)
        e2.append(jnp.exp(s2_blocks[k] - t2[0]))
    n1_ref[0] = jnp.concatenate(n1, axis=0)
    rank2_ref[0] = jnp.concatenate(rank2, axis=0).astype(rank2_ref.dtype)
    e1_ref[0] = jnp.concatenate(e1, axis=0)
    e2_ref[0] = jnp.concatenate(e2, axis=0).astype(e2_ref.dtype)


def _peer_route(qp, sub_keys, tm=512):
    t = qp.shape[0]
    key_out = pl.BlockSpec((1, N_KEYS, tm), lambda i, h: (h, 0, i))
    key_shape = lambda dtype: jax.ShapeDtypeStruct((PEER_HEADS, N_KEYS, t), dtype)
    return pl.pallas_call(
        _route_kernel,
        grid=(t // tm, PEER_HEADS),
        in_specs=[pl.BlockSpec((tm, 2 * PEER_HALF), lambda i, h: (i, h)),
                  pl.BlockSpec((1, 2, N_KEYS, PEER_HALF), lambda i, h: (h, 0, 0, 0))],
        out_specs=[key_out, key_out, key_out, key_out],
        out_shape=[key_shape(F32), key_shape(BF16), key_shape(F32), key_shape(BF16)],
        compiler_params=_params("parallel", "parallel"),
        name="peer_route",
    )(qp, sub_keys)


def _peer_kernel(xt_ref, u_ref, vt_ref, n1_ref, rank2_ref, e1_ref, e2_ref, o_ref, *, n_grp):
    c = pl.program_id(1)

    @pl.when(c == 0)
    def _():
        o_ref[...] = jnp.zeros_like(o_ref)

    hid = jnp.dot(u_ref[...], xt_ref[...], preferred_element_type=F32)
    sqrt_half = 0.7071067811865476
    w_parts = []
    for g in range(n_grp):
        i = c * n_grp + g
        gate = None
        for h in range(PEER_HEADS):
            n1 = n1_ref[h, pl.ds(i, 1), :].astype(BF16)
            e1 = e1_ref[h, pl.ds(i, 1), :].astype(BF16)
            contrib = jnp.where(rank2_ref[h] < n1, e1 * e2_ref[h], jnp.zeros((), BF16))
            gate = contrib if gate is None else gate + contrib
        hg = hid[g * N_KEYS:(g + 1) * N_KEYS, :]
        act = 0.5 * hg * (1.0 + lax.erf(hg * sqrt_half))
        w_parts.append(act.astype(BF16) * gate)
    w = w_parts[0] if n_grp == 1 else jnp.concatenate(w_parts, axis=0)
    o_ref[...] += jnp.dot(vt_ref[...], w, preferred_element_type=F32)


def _peer_mix(xnt, u, vt, n1, rank2, e1, e2, tt=512, n_grp=4):
    d, t = xnt.shape
    ec = n_grp * N_KEYS
    once = pl.Buffered(1)
    tok = pl.BlockSpec((PEER_HEADS, N_KEYS, tt), lambda i, c: (0, 0, i), pipeline_mode=once)
    return pl.pallas_call(
        functools.partial(_peer_kernel, n_grp=n_grp),
        grid=(t // tt, N_EXPERTS // ec),
        in_specs=[pl.BlockSpec((d, tt), lambda i, c: (0, i), pipeline_mode=once),
                  pl.BlockSpec((ec, d), lambda i, c: (c, 0)),
                  pl.BlockSpec((d, ec), lambda i, c: (0, c)),
                  tok, tok, tok, tok],
        out_specs=pl.BlockSpec((d, tt), lambda i, c: (0, i)),
        out_shape=jax.ShapeDtypeStruct((d, t), F32),
        compiler_params=_params("parallel", "arbitrary"),
        name="peer_mix",
    )(xnt, u, vt, n1, rank2, e1, e2)


def kernel(x, norm_mix, w_in, b_gate, conv_w, w_conv_out, w_attn_out, w_o, norm_ffn,
           w_peer_q, sub_keys, u_emb, v_emb, norm_final):
    batch, seq, d = x.shape
    assert w_in.shape[0] == 1, "single-layer block"
    h = x.reshape(batch * seq, d)
    hn = _rmsnorm(h, norm_mix[0], BF16)
    bz = _conv_proj(hn, w_in[0], conv_w[0], seq)
    z = _matmul(hn, w_in[0], F32, col0=COL_Q, n=IN_COLS - COL_Q)
    attn, v_t = _moba_attention(z, v_emb[0], batch, seq)
    merged, u_bf = _merge(bz, attn, w_conv_out[0], w_attn_out[0], z, b_gate[0], u_emb[0])
    h = _matmul_residual(merged, w_o[0].astype(BF16), h)
    hn, hnt = _rmsnorm_with_transpose(h, norm_ffn[0])
    qp = _matmul(hn, w_peer_q[0], F32, col0=0, n=w_peer_q.shape[-1])
    n1, rank2, e1, e2 = _peer_route(qp, sub_keys[0])
    peer_t = _peer_mix(hnt, u_bf, v_t, n1, rank2, e1, e2)
    return _add_rmsnorm(h, peer_t, norm_final).reshape(batch, seq, d)
```

```python
import functools

import jax
import jax.numpy as jnp
from jax import lax
from jax.experimental import pallas as pl
from jax.experimental.pallas import tpu as pltpu

D_MODEL = 4096
CONV_DIM = 2048
CONV_WIDTH = 3
N_HEADS = 16
HEAD_DIM = 128
ATTN_DIM = N_HEADS * HEAD_DIM
MOBA_BLOCK = 256
MOBA_TOPK = 3
PEER_HEADS = 8
N_KEYS = 128
N_EXPERTS = N_KEYS * N_KEYS
PEER_HALF = 128
PEER_TOPK = 16
EPS = 1e-6

COL_B = 0
COL_C = CONV_DIM
COL_U = 2 * CONV_DIM
COL_Q = 3 * CONV_DIM
COL_K = COL_Q + ATTN_DIM
COL_V = COL_K + ATTN_DIM
COL_GC = COL_V + ATTN_DIM
COL_GA = COL_GC + D_MODEL
IN_COLS = COL_GA + D_MODEL
Z_Q = 0
Z_K = COL_K - COL_Q
Z_V = COL_V - COL_Q
Z_GC = COL_GC - COL_Q
Z_GA = COL_GA - COL_Q

VMEM_LIMIT_BYTES = 56 * 1024 * 1024

F32 = jnp.float32
BF16 = jnp.bfloat16
NEG_INF = float("-inf")
SUBLANES = 8


def _params(*semantics):
    return pltpu.CompilerParams(dimension_semantics=semantics,
                                vmem_limit_bytes=VMEM_LIMIT_BYTES)


def _rmsnorm_kernel(x_ref, g_ref, o_ref):
    x = x_ref[...]
    y = x * lax.rsqrt(jnp.mean(x * x, axis=-1, keepdims=True) + EPS)
    o_ref[...] = (y * g_ref[...]).astype(o_ref.dtype)


def _rmsnorm(x, g, out_dtype, tm=256):
    m, d = x.shape
    return pl.pallas_call(
        _rmsnorm_kernel,
        grid=(m // tm,),
        in_specs=[pl.BlockSpec((tm, d), lambda i: (i, 0)),
                  pl.BlockSpec((1, d), lambda i: (0, 0))],
        out_specs=pl.BlockSpec((tm, d), lambda i: (i, 0)),
        out_shape=jax.ShapeDtypeStruct((m, d), out_dtype),
        compiler_params=_params("parallel"),
        name="rmsnorm",
    )(x, g.reshape(1, d))


def _rmsnorm_t_kernel(x_ref, g_ref, o_ref, ot_ref):
    x = x_ref[...]
    y = x * lax.rsqrt(jnp.mean(x * x, axis=-1, keepdims=True) + EPS) * g_ref[...]
    o_ref[...] = y.astype(o_ref.dtype)
    ot_ref[...] = y.T.astype(ot_ref.dtype)


def _rmsnorm_with_transpose(x, g, tm=256):
    m, d = x.shape
    return pl.pallas_call(
        _rmsnorm_t_kernel,
        grid=(m // tm,),
        in_specs=[pl.BlockSpec((tm, d), lambda i: (i, 0)),
                  pl.BlockSpec((1, d), lambda i: (0, 0))],
        out_specs=[pl.BlockSpec((tm, d), lambda i: (i, 0)),
                   pl.BlockSpec((d, tm), lambda i: (0, i))],
        out_shape=[jax.ShapeDtypeStruct((m, d), BF16),
                   jax.ShapeDtypeStruct((d, m), BF16)],
        compiler_params=_params("parallel"),
        name="rmsnorm_t",
    )(x, g.reshape(1, d))


def _add_rmsnorm_kernel(a_ref, bt_ref, g_ref, o_ref):
    x = a_ref[...] + bt_ref[...].T
    y = x * lax.rsqrt(jnp.mean(x * x, axis=-1, keepdims=True) + EPS)
    o_ref[...] = (y * g_ref[...]).astype(o_ref.dtype)


def _add_rmsnorm(a, bt, g, tm=256):
    m, d = a.shape
    return pl.pallas_call(
        _add_rmsnorm_kernel,
        grid=(m // tm,),
        in_specs=[pl.BlockSpec((tm, d), lambda i: (i, 0)),
                  pl.BlockSpec((d, tm), lambda i: (0, i)),
                  pl.BlockSpec((1, d), lambda i: (0, 0))],
        out_specs=pl.BlockSpec((tm, d), lambda i: (i, 0)),
        out_shape=jax.ShapeDtypeStruct((m, d), F32),
        compiler_params=_params("parallel"),
        name="add_rmsnorm",
    )(a, bt, g.reshape(1, d))


def _cast_at_first_row_tile(pairs):
    @pl.when(pl.program_id(1) == 0)
    def _():
        for src, dst in pairs:
            dst[...] = src[...].astype(dst.dtype)


def _matmul_kernel(a_ref, w_ref, o_ref, wb_ref):
    _cast_at_first_row_tile([(w_ref, wb_ref)])
    o_ref[...] = jnp.dot(a_ref[...], wb_ref[...],
                         preferred_element_type=F32).astype(o_ref.dtype)


def _matmul(a, w, out_dtype, col0, n, tm=512, tn=1024):
    m, k = a.shape
    return pl.pallas_call(
        _matmul_kernel,
        grid=(n // tn, m // tm),
        in_specs=[pl.BlockSpec((tm, k), lambda j, i: (i, 0)),
                  pl.BlockSpec((k, tn), lambda j, i: (0, col0 // tn + j))],
        out_specs=pl.BlockSpec((tm, tn), lambda j, i: (i, j)),
        out_shape=jax.ShapeDtypeStruct((m, n), out_dtype),
        scratch_shapes=[pltpu.VMEM((k, tn), BF16)],
        compiler_params=_params("parallel", "arbitrary"),
        name="matmul",
    )(a, w)


def _matmul_residual_kernel(a_ref, w_ref, r_ref, o_ref):
    o_ref[...] = r_ref[...] + jnp.dot(a_ref[...], w_ref[...], preferred_element_type=F32)


def _matmul_residual(a, w, r, tm=1024, tn=1024):
    m, k = a.shape
    _, n = w.shape
    return pl.pallas_call(
        _matmul_residual_kernel,
        grid=(n // tn, m // tm),
        in_specs=[pl.BlockSpec((tm, k), lambda j, i: (i, 0)),
                  pl.BlockSpec((k, tn), lambda j, i: (0, j)),
                  pl.BlockSpec((tm, tn), lambda j, i: (i, j))],
        out_specs=pl.BlockSpec((tm, tn), lambda j, i: (i, j)),
        out_shape=jax.ShapeDtypeStruct((m, n), F32),
        compiler_params=_params("parallel", "parallel"),
        name="matmul_residual",
    )(a, w, r)


def _conv_proj_kernel(a_ref, wb_ref, wc_ref, wu_ref, cw_ref, o_ref,
                      wb_bf, wc_bf, wu_bf, hist_ref, *, tiles_per_seq):
    _cast_at_first_row_tile([(wb_ref, wb_bf), (wc_ref, wc_bf), (wu_ref, wu_bf)])

    @pl.when(pl.program_id(1) % tiles_per_seq == 0)
    def _():
        hist_ref[...] = jnp.zeros_like(hist_ref)

    a = a_ref[...]
    z = (jnp.dot(a, wc_bf[...], preferred_element_type=F32)
         * jnp.dot(a, wu_bf[...], preferred_element_type=F32))
    hist = hist_ref[...]
    n_hist = hist.shape[0]
    row = lax.broadcasted_iota(jnp.int32, hist.shape, 0)
    w = cw_ref[...]
    acc = z * w[2:3, :]
    for shift in (1, 2):
        zs = pltpu.roll(z, shift, axis=0)
        top = jnp.where(row >= shift, zs[:n_hist], pltpu.roll(hist, shift, axis=0))
        zs = jnp.concatenate([top, zs[n_hist:]], axis=0)
        acc = acc + zs * w[2 - shift:3 - shift, :]
    hist_ref[...] = z[z.shape[0] - n_hist:]
    o_ref[...] = (jnp.dot(a, wb_bf[...], preferred_element_type=F32) * acc).astype(o_ref.dtype)


def _conv_proj(a, w_in, conv_w, seq, tm=1024, tn=256):
    m, k = a.shape
    nb = CONV_DIM // tn
    wspec = lambda off: pl.BlockSpec((k, tn), lambda j, i: (0, off // tn + j))
    return pl.pallas_call(
        functools.partial(_conv_proj_kernel, tiles_per_seq=seq // tm),
        grid=(nb, m // tm),
        in_specs=[pl.BlockSpec((tm, k), lambda j, i: (i, 0)),
                  wspec(COL_B), wspec(COL_C), wspec(COL_U),
                  pl.BlockSpec((CONV_WIDTH, tn), lambda j, i: (0, j))],
        out_specs=pl.BlockSpec((tm, tn), lambda j, i: (i, j)),
        out_shape=jax.ShapeDtypeStruct((m, CONV_DIM), BF16),
        scratch_shapes=[pltpu.VMEM((k, tn), BF16)] * 3 + [pltpu.VMEM((SUBLANES, tn), F32)],
        compiler_params=_params("parallel", "arbitrary"),
        name="conv_proj",
    )(a, w_in, w_in, w_in, conv_w)


def _moba_kernel(q_ref, k_ref, v_ref, tab_ref, ca_ref, cb_ref, o_ref, tab_t_ref, ca_bf_ref, cb_bf_ref,
                 *, n_blk):
    tab_t_ref[...] = tab_ref[...].T.astype(tab_t_ref.dtype)
    ca_bf_ref[...] = ca_ref[...].astype(ca_bf_ref.dtype)
    cb_bf_ref[...] = cb_ref[...].astype(cb_bf_ref.dtype)
    seq = q_ref.shape[0]
    nt_dims = (((1,), (1,)), ((), ()))
    k = k_ref[...]
    kb = k.astype(BF16)
    vb = v_ref[...].astype(BF16)
    q_all = q_ref[...]
    k_mean = jnp.mean(k.reshape(n_blk, MOBA_BLOCK, HEAD_DIM), axis=1)

    gate = lax.dot_general(k_mean, q_all, nt_dims, precision=lax.Precision.HIGHEST,
                           preferred_element_type=F32)
    blk = lax.broadcasted_iota(jnp.int32, (n_blk, seq), 0)
    own = lax.broadcasted_iota(jnp.int32, (n_blk, seq), 1) // MOBA_BLOCK
    past = blk < own
    gate = jnp.where(past, gate, NEG_INF)
    rank = jnp.zeros((n_blk, seq), jnp.int32)
    for jp in range(n_blk - 1):
        gj = gate[jp:jp + 1, :]
        ahead = (gj > gate) | ((gj == gate) & (blk > jp))
        rank = rank + ahead.astype(jnp.int32)
    sel = jnp.where((rank < MOBA_TOPK) & past, 1.0, 0.0).astype(BF16)
    expand = jnp.where(lax.broadcasted_iota(jnp.int32, (n_blk, seq), 1) // MOBA_BLOCK == blk,
                       1.0, 0.0).astype(BF16)

    row = lax.broadcasted_iota(jnp.int32, (MOBA_BLOCK, MOBA_BLOCK), 0)
    col = lax.broadcasted_iota(jnp.int32, (MOBA_BLOCK, MOBA_BLOCK), 1)
    causal = col <= row
    q_scale = HEAD_DIM ** -0.5 * 1.4426950408889634
    for i in range(n_blk):
        lo, nk = i * MOBA_BLOCK, (i + 1) * MOBA_BLOCK
        q = (q_all[lo:nk] * q_scale).astype(BF16)
        s = lax.dot_general(q, kb[:nk], nt_dims, preferred_element_type=F32)
        s_own = jnp.where(causal, s[:, lo:nk], NEG_INF)
        if i > 0:
            keep = lax.dot_general(sel[:, lo:nk], expand[:, :lo], (((0,), (0,)), ((), ())),
                                   preferred_element_type=F32)
            s = jnp.concatenate([jnp.where(keep > 0.5, s[:, :lo], NEG_INF), s_own], axis=1)
        else:
            s = s_own
        m = jnp.max(s, axis=-1, keepdims=True)
        p = jnp.exp2(s - m)
        l = jnp.sum(p, axis=-1, keepdims=True)
        out = jnp.dot(p.astype(BF16), vb[:nk], preferred_element_type=F32) / l
        o_ref[lo:nk, :] = out.astype(o_ref.dtype)


def _moba_attention(z, table, cast_a, cast_b, batch, seq):
    n_blk = seq // MOBA_BLOCK
    steps = batch * N_HEADS
    step = lambda b, h: b * N_HEADS + h
    slab = lambda x: x.shape[0] // steps
    row_slab = lambda x: pl.BlockSpec((slab(x), x.shape[1]), lambda b, h: (step(b, h), 0))
    spec = lambda off: pl.BlockSpec((seq, HEAD_DIM), lambda b, h: (b, off // HEAD_DIM + h))
    rows, cols = table.shape
    return pl.pallas_call(
        functools.partial(_moba_kernel, n_blk=n_blk),
        grid=(batch, N_HEADS),
        in_specs=[spec(Z_Q), spec(Z_K), spec(Z_V), row_slab(table), row_slab(cast_a), row_slab(cast_b)],
        out_specs=[pl.BlockSpec((seq, HEAD_DIM), lambda b, h: (b, h)),
                   pl.BlockSpec((cols, slab(table)), lambda b, h: (0, step(b, h))),
                   row_slab(cast_a), row_slab(cast_b)],
        out_shape=[jax.ShapeDtypeStruct((batch * seq, ATTN_DIM), BF16),
                   jax.ShapeDtypeStruct((cols, rows), BF16),
                   jax.ShapeDtypeStruct(cast_a.shape, BF16),
                   jax.ShapeDtypeStruct(cast_b.shape, BF16)],
        compiler_params=_params("parallel", "parallel"),
        name="moba_attention",
    )(z, z, z, table, cast_a, cast_b)


def _merge_kernel(bz_ref, at_ref, wc_ref, wa_ref, gc_ref, ga_ref, bc_ref, ba_ref, o_ref,
                  wc_bf, wa_bf):
    _cast_at_first_row_tile([(wc_ref, wc_bf), (wa_ref, wa_bf)])
    y_conv = jnp.dot(bz_ref[...], wc_bf[...], preferred_element_type=F32)
    y_attn = jnp.dot(at_ref[...], wa_bf[...], preferred_element_type=F32)
    merged = (jax.nn.sigmoid(gc_ref[...] + bc_ref[...]) * y_conv
              + jax.nn.sigmoid(ga_ref[...] + ba_ref[...]) * y_attn)
    o_ref[...] = merged.astype(o_ref.dtype)


def _merge(bz, attn, w_conv_out, w_attn_out, z, b_gate, tm=1024, tn=512):
    m = bz.shape[0]
    nb = D_MODEL // tn
    bias = b_gate.reshape(1, 2 * D_MODEL)
    return pl.pallas_call(
        _merge_kernel,
        grid=(nb, m // tm),
        in_specs=[pl.BlockSpec((tm, CONV_DIM), lambda j, i: (i, 0)),
                  pl.BlockSpec((tm, ATTN_DIM), lambda j, i: (i, 0)),
                  pl.BlockSpec((CONV_DIM, tn), lambda j, i: (0, j)),
                  pl.BlockSpec((ATTN_DIM, tn), lambda j, i: (0, j)),
                  pl.BlockSpec((tm, tn), lambda j, i: (i, Z_GC // tn + j)),
                  pl.BlockSpec((tm, tn), lambda j, i: (i, Z_GA // tn + j)),
                  pl.BlockSpec((1, tn), lambda j, i: (0, j)),
                  pl.BlockSpec((1, tn), lambda j, i: (0, nb + j))],
        out_specs=pl.BlockSpec((tm, tn), lambda j, i: (i, j)),
        out_shape=jax.ShapeDtypeStruct((m, D_MODEL), BF16),
        scratch_shapes=[pltpu.VMEM((CONV_DIM, tn), BF16), pltpu.VMEM((ATTN_DIM, tn), BF16)],
        compiler_params=_params("parallel", "arbitrary"),
        name="merge",
    )(bz, attn, w_conv_out, w_attn_out, z, z, bias, bias)


def _sort16_pairs():
    pairs = []

    def merge(lo, n, r):
        step = r * 2
        if step < n:
            merge(lo, n, step)
            merge(lo + r, n, step)
            pairs.extend((i, i + r) for i in range(lo + r, lo + n - r, step))
        else:
            pairs.append((lo, lo + r))

    def sort(lo, n):
        if n > 1:
            sort(lo, n // 2)
            sort(lo + n // 2, n // 2)
            merge(lo, n, 1)

    sort(0, PEER_TOPK)
    return pairs


_SORT16 = _sort16_pairs()


def _compare_exchange(vals, i, j):
    a, b = vals[i], vals[j]
    if b is None:
        return
    if a is None:
        vals[i], vals[j] = b, None
        return
    vals[i], vals[j] = jnp.maximum(a, b), jnp.minimum(a, b)


def _top16(blocks):
    vals = list(blocks) + [None] * (PEER_TOPK - len(blocks))
    for i, j in _SORT16:
        _compare_exchange(vals, i, j)
    for shift in (4, 2, 1):
        other = [None if v is None else pltpu.roll(v, shift, axis=0) for v in vals]
        merged = []
        for i in range(PEER_TOPK):
            a, b = vals[i], other[PEER_TOPK - 1 - i]
            merged.append(b if a is None else a if b is None else jnp.maximum(a, b))
        vals = merged
        for step in (8, 4, 2, 1):
            for i in range(PEER_TOPK):
                if not i & step:
                    _compare_exchange(vals, i, i + step)
    return vals


def _sublane_sum(x):
    for shift in (4, 2, 1):
        x = x + pltpu.roll(x, shift, axis=0)
    return x


def _route_kernel(q_ref, sk_ref, n1_ref, rank2_ref, e1_ref, e2_ref):
    q = q_ref[...]
    tm = q.shape[0]
    dims = (((1,), (1,)), ((), ()))
    s1 = lax.dot_general(sk_ref[0, 0], q[:, :PEER_HALF], dims,
                         precision=lax.Precision.HIGHEST, preferred_element_type=F32)
    s2 = lax.dot_general(sk_ref[0, 1], q[:, PEER_HALF:], dims,
                         precision=lax.Precision.HIGHEST, preferred_element_type=F32)
    n_blocks = N_KEYS // SUBLANES
    s1_blocks = [s1[k * SUBLANES:(k + 1) * SUBLANES, :] for k in range(n_blocks)]
    s2_blocks = [s2[k * SUBLANES:(k + 1) * SUBLANES, :] for k in range(n_blocks)]
    t1 = _top16(s1_blocks)
    t2 = _top16(s2_blocks)

    sub = lax.broadcasted_iota(jnp.int32, (SUBLANES, tm), 0)
    spread = lambda rows: functools.reduce(
        lambda acc, b: jnp.where(sub == b, rows[b], acc), range(SUBLANES - 1), rows[SUBLANES - 1])
    t2_lo, t2_hi, t1_hi = spread(t2[:SUBLANES]), spread(t2[SUBLANES:]), spread(t1[SUBLANES:])
    cand = [t1[0] + t2_lo, t1[0] + t2_hi]
    for a in range(1, SUBLANES):
        n_valid = PEER_TOPK // (a + 1)
        piece = t1[a] + t2_lo
        cand.append(piece if n_valid >= SUBLANES else jnp.where(sub < n_valid, piece, NEG_INF))
    cand.append(t1_hi + t2[0])
    best = _top16(cand)
    tau = best[PEER_TOPK - 1]
    z = jnp.ones_like(tau)
    for r in range(1, PEER_TOPK):
        z = z + jnp.exp(best[r] - best[0])
    inv_z = 1.0 / z

    count = [_sublane_sum(jnp.where(t1[a] + t2_lo >= tau, 1.0, 0.0)
                          + jnp.where(t1[a] + t2_hi >= tau, 1.0, 0.0)) for a in range(PEER_TOPK)]
    n1, rank2, e1, e2 = [], [], [], []
    for k in range(n_blocks):
        n1_k = jnp.zeros((SUBLANES, tm), F32)
        rank2_k = jnp.full((SUBLANES, tm), float(PEER_TOPK), F32)
        for a in reversed(range(PEER_TOPK)):
            n1_k = jnp.where(s1_blocks[k] == t1[a], count[a], n1_k)
            rank2_k = jnp.where(s2_blocks[k] == t2[a], float(a), rank2_k)
        n1.append(n1_k)
        rank2.append(rank2_k)
        e1.append(jnp.exp(s1_blocks[k] - t1[0]) * inv_z)
        e2.append(jnp.exp(s2_blocks[k] - t2[0]))
    n1_ref[0] = jnp.concatenate(n1, axis=0)
    rank2_ref[0] = jnp.concatenate(rank2, axis=0).astype(rank2_ref.dtype)
    e1_ref[0] = jnp.concatenate(e1, axis=0)
    e2_ref[0] = jnp.concatenate(e2, axis=0).astype(e2_ref.dtype)


def _peer_route(qp, sub_keys, tm=512):
    t = qp.shape[0]
    key_out = pl.BlockSpec((1, N_KEYS, tm), lambda i, h: (h, 0, i))
    key_shape = lambda dtype: jax.ShapeDtypeStruct((PEER_HEADS, N_KEYS, t), dtype)
    return pl.pallas_call(
        _route_kernel,
        grid=(t // tm, PEER_HEADS),
        in_specs=[pl.BlockSpec((tm, 2 * PEER_HALF), lambda i, h: (i, h)),
                  pl.BlockSpec((1, 2, N_KEYS, PEER_HALF), lambda i, h: (h, 0, 0, 0))],
        out_specs=[key_out, key_out, key_out, key_out],
        out_shape=[key_shape(F32), key_shape(BF16), key_shape(F32), key_shape(BF16)],
        compiler_params=_params("parallel", "parallel"),
        name="peer_route",
    )(qp, sub_keys)


def _peer_kernel(xt_ref, u_ref, vt_ref, n1_ref, rank2_ref, e1_ref, e2_ref, o_ref, *, n_grp):
    c = pl.program_id(1)

    @pl.when(c == 0)
    def _():
        o_ref[...] = jnp.zeros_like(o_ref)

    hid = jnp.dot(u_ref[...], xt_ref[...], preferred_element_type=F32)
    sqrt_half = 0.7071067811865476
    tt = hid.shape[1]
    pack = 2 * SUBLANES
    w_parts = []
    for g in range(n_grp):
        i = c * n_grp + g
        spread = lambda ref, h: jnp.broadcast_to(ref[h, pl.ds(i, 1), :], (pack, tt)).astype(BF16)
        n1 = [spread(n1_ref, h) for h in range(PEER_HEADS)]
        e1 = [spread(e1_ref, h) for h in range(PEER_HEADS)]
        for lo in range(0, N_KEYS, pack):
            keys = slice(lo, lo + pack)
            gate = None
            for h in range(PEER_HEADS):
                contrib = jnp.where(rank2_ref[h, keys, :] < n1[h], e1[h] * e2_ref[h, keys, :],
                                    jnp.zeros((), BF16))
                gate = contrib if gate is None else gate + contrib
            hg = hid[g * N_KEYS + lo:g * N_KEYS + lo + pack, :]
            act = 0.5 * hg * (1.0 + lax.erf(hg * sqrt_half))
            w_parts.append(act.astype(BF16) * gate)
    w = jnp.concatenate(w_parts, axis=0)
    o_ref[...] += jnp.dot(vt_ref[...], w, preferred_element_type=F32)


def _peer_mix(xnt, u, vt, n1, rank2, e1, e2, tt=512, n_grp=4):
    d, t = xnt.shape
    ec = n_grp * N_KEYS
    once = pl.Buffered(1)
    tok = pl.BlockSpec((PEER_HEADS, N_KEYS, tt), lambda i, c: (0, 0, i), pipeline_mode=once)
    return pl.pallas_call(
        functools.partial(_peer_kernel, n_grp=n_grp),
        grid=(t // tt, N_EXPERTS // ec),
        in_specs=[pl.BlockSpec((d, tt), lambda i, c: (0, i), pipeline_mode=once),
                  pl.BlockSpec((ec, d), lambda i, c: (c, 0)),
                  pl.BlockSpec((d, ec), lambda i, c: (0, c)),
                  tok, tok, tok, tok],
        out_specs=pl.BlockSpec((d, tt), lambda i, c: (0, i)),
        out_shape=jax.ShapeDtypeStruct((d, t), F32),
        compiler_params=_params("parallel", "arbitrary"),
        name="peer_mix",
    )(xnt, u, vt, n1, rank2, e1, e2)


def kernel(x, norm_mix, w_in, b_gate, conv_w, w_conv_out, w_attn_out, w_o, norm_ffn,
           w_peer_q, sub_keys, u_emb, v_emb, norm_final):
    batch, seq, d = x.shape
    assert w_in.shape[0] == 1, "single-layer block"
    h = x.reshape(batch * seq, d)
    hn = _rmsnorm(h, norm_mix[0], BF16)
    bz = _conv_proj(hn, w_in[0], conv_w[0], seq)
    z = _matmul(hn, w_in[0], F32, col0=COL_Q, n=IN_COLS - COL_Q)
    attn, v_t, u_bf, w_o_bf = _moba_attention(z, v_emb[0], u_emb[0], w_o[0], batch, seq)
    merged = _merge(bz, attn, w_conv_out[0], w_attn_out[0], z, b_gate[0])
    h = _matmul_residual(merged, w_o_bf, h)
    hn, hnt = _rmsnorm_with_transpose(h, norm_ffn[0])
    qp = _matmul(hn, w_peer_q[0], F32, col0=0, n=w_peer_q.shape[-1])
    n1, rank2, e1, e2 = _peer_route(qp, sub_keys[0])
    peer_t = _peer_mix(hnt, u_bf, v_t, n1, rank2, e1, e2)
    return _add_rmsnorm(h, peer_t, norm_final).reshape(batch, seq, d)
```

```python
import functools

import jax
import jax.numpy as jnp
from jax import lax
from jax.experimental import pallas as pl
from jax.experimental.pallas import tpu as pltpu

D_MODEL = 4096
CONV_DIM = 2048
CONV_WIDTH = 3
N_HEADS = 16
HEAD_DIM = 128
ATTN_DIM = N_HEADS * HEAD_DIM
MOBA_BLOCK = 256
MOBA_TOPK = 3
PEER_HEADS = 8
N_KEYS = 128
N_EXPERTS = N_KEYS * N_KEYS
PEER_HALF = 128
PEER_TOPK = 16
EPS = 1e-6

COL_B = 0
COL_C = CONV_DIM
COL_U = 2 * CONV_DIM
COL_Q = 3 * CONV_DIM
COL_K = COL_Q + ATTN_DIM
COL_V = COL_K + ATTN_DIM
COL_GC = COL_V + ATTN_DIM
COL_GA = COL_GC + D_MODEL
IN_COLS = COL_GA + D_MODEL
Z_Q = 0
Z_K = COL_K - COL_Q
Z_V = COL_V - COL_Q
Z_GC = COL_GC - COL_Q
Z_GA = COL_GA - COL_Q

VMEM_LIMIT_BYTES = 56 * 1024 * 1024

F32 = jnp.float32
BF16 = jnp.bfloat16
NEG_INF = float("-inf")
SUBLANES = 8


def _params(*semantics):
    return pltpu.CompilerParams(dimension_semantics=semantics,
                                vmem_limit_bytes=VMEM_LIMIT_BYTES)


def _rmsnorm_kernel(x_ref, g_ref, o_ref):
    x = x_ref[...]
    y = x * lax.rsqrt(jnp.mean(x * x, axis=-1, keepdims=True) + EPS)
    o_ref[...] = (y * g_ref[...]).astype(o_ref.dtype)


def _rmsnorm(x, g, out_dtype, tm=256):
    m, d = x.shape
    return pl.pallas_call(
        _rmsnorm_kernel,
        grid=(m // tm,),
        in_specs=[pl.BlockSpec((tm, d), lambda i: (i, 0)),
                  pl.BlockSpec((1, d), lambda i: (0, 0))],
        out_specs=pl.BlockSpec((tm, d), lambda i: (i, 0)),
        out_shape=jax.ShapeDtypeStruct((m, d), out_dtype),
        compiler_params=_params("parallel"),
        name="rmsnorm",
    )(x, g.reshape(1, d))


def _rmsnorm_t_kernel(x_ref, g_ref, o_ref, ot_ref):
    x = x_ref[...]
    y = x * lax.rsqrt(jnp.mean(x * x, axis=-1, keepdims=True) + EPS) * g_ref[...]
    o_ref[...] = y.astype(o_ref.dtype)
    ot_ref[0] = y.T.astype(ot_ref.dtype)


def _rmsnorm_with_transpose(x, g, tm):
    m, d = x.shape
    return pl.pallas_call(
        _rmsnorm_t_kernel,
        grid=(m // tm,),
        in_specs=[pl.BlockSpec((tm, d), lambda i: (i, 0)),
                  pl.BlockSpec((1, d), lambda i: (0, 0))],
        out_specs=[pl.BlockSpec((tm, d), lambda i: (i, 0)),
                   pl.BlockSpec((1, d, tm), lambda i: (i, 0, 0))],
        out_shape=[jax.ShapeDtypeStruct((m, d), BF16),
                   jax.ShapeDtypeStruct((m // tm, d, tm), BF16)],
        compiler_params=_params("parallel"),
        name="rmsnorm_t",
    )(x, g.reshape(1, d))


def _add_rmsnorm_kernel(a_ref, bt_ref, g_ref, o_ref, *, parts):
    tm = a_ref.shape[0]
    start = pl.multiple_of((pl.program_id(0) % parts) * tm, tm)
    x = a_ref[...] + bt_ref[0, :, pl.ds(start, tm)].T
    y = x * lax.rsqrt(jnp.mean(x * x, axis=-1, keepdims=True) + EPS)
    o_ref[...] = (y * g_ref[...]).astype(o_ref.dtype)


def _add_rmsnorm(a, bt, g, parts=2):
    m, d = a.shape
    tile = bt.shape[2]
    tm = tile // parts
    return pl.pallas_call(
        functools.partial(_add_rmsnorm_kernel, parts=parts),
        grid=(m // tm,),
        in_specs=[pl.BlockSpec((tm, d), lambda i: (i, 0)),
                  pl.BlockSpec((1, d, tile), lambda i: (i // parts, 0, 0)),
                  pl.BlockSpec((1, d), lambda i: (0, 0))],
        out_specs=pl.BlockSpec((tm, d), lambda i: (i, 0)),
        out_shape=jax.ShapeDtypeStruct((m, d), F32),
        compiler_params=_params("parallel"),
        name="add_rmsnorm",
    )(a, bt, g.reshape(1, d))


def _cast_at_first_row_tile(pairs):
    @pl.when(pl.program_id(1) == 0)
    def _():
        for src, dst in pairs:
            dst[...] = src[...].astype(dst.dtype)


def _matmul_kernel(a_ref, w_ref, o_ref, wb_ref):
    _cast_at_first_row_tile([(w_ref, wb_ref)])
    o_ref[...] = jnp.dot(a_ref[...], wb_ref[...],
                         preferred_element_type=F32).astype(o_ref.dtype)


def _matmul(a, w, out_dtype, col0, n, tm=512, tn=1024):
    m, k = a.shape
    return pl.pallas_call(
        _matmul_kernel,
        grid=(n // tn, m // tm),
        in_specs=[pl.BlockSpec((tm, k), lambda j, i: (i, 0)),
                  pl.BlockSpec((k, tn), lambda j, i: (0, col0 // tn + j))],
        out_specs=pl.BlockSpec((tm, tn), lambda j, i: (i, j)),
        out_shape=jax.ShapeDtypeStruct((m, n), out_dtype),
        scratch_shapes=[pltpu.VMEM((k, tn), BF16)],
        compiler_params=_params("parallel", "arbitrary"),
        name="matmul",
    )(a, w)


def _matmul_residual_kernel(a_ref, w_ref, r_ref, o_ref):
    o_ref[...] = r_ref[...] + jnp.dot(a_ref[...], w_ref[...], preferred_element_type=F32)


def _matmul_residual(a, w, r, tm=1024, tn=1024):
    m, k = a.shape
    _, n = w.shape
    return pl.pallas_call(
        _matmul_residual_kernel,
        grid=(n // tn, m // tm),
        in_specs=[pl.BlockSpec((tm, k), lambda j, i: (i, 0)),
                  pl.BlockSpec((k, tn), lambda j, i: (0, j)),
                  pl.BlockSpec((tm, tn), lambda j, i: (i, j))],
        out_specs=pl.BlockSpec((tm, tn), lambda j, i: (i, j)),
        out_shape=jax.ShapeDtypeStruct((m, n), F32),
        compiler_params=_params("parallel", "parallel"),
        name="matmul_residual",
    )(a, w, r)


def _conv_proj_kernel(a_ref, wb_ref, wc_ref, wu_ref, cw_ref, o_ref,
                      wb_bf, wc_bf, wu_bf, hist_ref, *, tiles_per_seq):
    _cast_at_first_row_tile([(wb_ref, wb_bf), (wc_ref, wc_bf), (wu_ref, wu_bf)])

    @pl.when(pl.program_id(1) % tiles_per_seq == 0)
    def _():
        hist_ref[...] = jnp.zeros_like(hist_ref)

    a = a_ref[...]
    z = (jnp.dot(a, wc_bf[...], preferred_element_type=F32)
         * jnp.dot(a, wu_bf[...], preferred_element_type=F32))
    hist = hist_ref[...]
    n_hist = hist.shape[0]
    row = lax.broadcasted_iota(jnp.int32, hist.shape, 0)
    w = cw_ref[...]
    acc = z * w[2:3, :]
    for shift in (1, 2):
        zs = pltpu.roll(z, shift, axis=0)
        top = jnp.where(row >= shift, zs[:n_hist], pltpu.roll(hist, shift, axis=0))
        zs = jnp.concatenate([top, zs[n_hist:]], axis=0)
        acc = acc + zs * w[2 - shift:3 - shift, :]
    hist_ref[...] = z[z.shape[0] - n_hist:]
    o_ref[...] = (jnp.dot(a, wb_bf[...], preferred_element_type=F32) * acc).astype(o_ref.dtype)


def _conv_proj(a, w_in, conv_w, seq, tm=1024, tn=256):
    m, k = a.shape
    nb = CONV_DIM // tn
    wspec = lambda off: pl.BlockSpec((k, tn), lambda j, i: (0, off // tn + j))
    return pl.pallas_call(
        functools.partial(_conv_proj_kernel, tiles_per_seq=seq // tm),
        grid=(nb, m // tm),
        in_specs=[pl.BlockSpec((tm, k), lambda j, i: (i, 0)),
                  wspec(COL_B), wspec(COL_C), wspec(COL_U),
                  pl.BlockSpec((CONV_WIDTH, tn), lambda j, i: (0, j))],
        out_specs=pl.BlockSpec((tm, tn), lambda j, i: (i, j)),
        out_shape=jax.ShapeDtypeStruct((m, CONV_DIM), BF16),
        scratch_shapes=[pltpu.VMEM((k, tn), BF16)] * 3 + [pltpu.VMEM((SUBLANES, tn), F32)],
        compiler_params=_params("parallel", "arbitrary"),
        name="conv_proj",
    )(a, w_in, w_in, w_in, conv_w)


def _moba_kernel(q_ref, k_ref, v_ref, tab_ref, ca_ref, cb_ref, o_ref, tab_t_ref, ca_bf_ref, cb_bf_ref,
                 *, n_blk, steps_per_chunk):
    @pl.when((pl.program_id(0) * pl.num_programs(1) + pl.program_id(1)) % steps_per_chunk == 0)
    def _():
        tab_t_ref[0] = tab_ref[...].T.astype(tab_t_ref.dtype)

    ca_bf_ref[...] = ca_ref[...].astype(ca_bf_ref.dtype)
    cb_bf_ref[...] = cb_ref[...].astype(cb_bf_ref.dtype)
    seq = q_ref.shape[0]
    nt_dims = (((1,), (1,)), ((), ()))
    k = k_ref[...]
    kb = k.astype(BF16)
    vb = v_ref[...].astype(BF16)
    q_all = q_ref[...]
    k_mean = jnp.mean(k.reshape(n_blk, MOBA_BLOCK, HEAD_DIM), axis=1)

    gate = lax.dot_general(k_mean, q_all, nt_dims, precision=lax.Precision.HIGHEST,
                           preferred_element_type=F32)
    blk = lax.broadcasted_iota(jnp.int32, (n_blk, seq), 0)
    own = lax.broadcasted_iota(jnp.int32, (n_blk, seq), 1) // MOBA_BLOCK
    past = blk < own
    gate = jnp.where(past, gate, NEG_INF)
    rank = jnp.zeros((n_blk, seq), jnp.int32)
    for jp in range(n_blk - 1):
        gj = gate[jp:jp + 1, :]
        ahead = (gj > gate) | ((gj == gate) & (blk > jp))
        rank = rank + ahead.astype(jnp.int32)
    sel = jnp.where((rank < MOBA_TOPK) & past, 1.0, 0.0).astype(BF16)
    expand = jnp.where(lax.broadcasted_iota(jnp.int32, (n_blk, seq), 1) // MOBA_BLOCK == blk,
                       1.0, 0.0).astype(BF16)

    row = lax.broadcasted_iota(jnp.int32, (MOBA_BLOCK, MOBA_BLOCK), 0)
    col = lax.broadcasted_iota(jnp.int32, (MOBA_BLOCK, MOBA_BLOCK), 1)
    causal = col <= row
    q_scale = HEAD_DIM ** -0.5 * 1.4426950408889634
    for i in range(n_blk):
        lo, nk = i * MOBA_BLOCK, (i + 1) * MOBA_BLOCK
        q = (q_all[lo:nk] * q_scale).astype(BF16)
        s = lax.dot_general(q, kb[:nk], nt_dims, preferred_element_type=F32)
        s_own = jnp.where(causal, s[:, lo:nk], NEG_INF)
        if i > 0:
            keep = lax.dot_general(sel[:, lo:nk], expand[:, :lo], (((0,), (0,)), ((), ())),
                                   preferred_element_type=F32)
            s = jnp.concatenate([jnp.where(keep > 0.5, s[:, :lo], NEG_INF), s_own], axis=1)
        else:
            s = s_own
        m = jnp.max(s, axis=-1, keepdims=True)
        p = jnp.exp2(s - m)
        l = jnp.sum(p, axis=-1, keepdims=True)
        out = jnp.dot(p.astype(BF16), vb[:nk], preferred_element_type=F32) / l
        o_ref[lo:nk, :] = out.astype(o_ref.dtype)


def _moba_attention(z, table, chunk, cast_a, cast_b, batch, seq):
    n_blk = seq // MOBA_BLOCK
    steps = batch * N_HEADS
    step = lambda b, h: b * N_HEADS + h
    slab = lambda x: x.shape[0] // steps
    row_slab = lambda x: pl.BlockSpec((slab(x), x.shape[1]), lambda b, h: (step(b, h), 0))
    spec = lambda off: pl.BlockSpec((seq, HEAD_DIM), lambda b, h: (b, off // HEAD_DIM + h))
    rows, cols = table.shape
    steps_per_chunk = steps * chunk // rows
    return pl.pallas_call(
        functools.partial(_moba_kernel, n_blk=n_blk, steps_per_chunk=steps_per_chunk),
        grid=(batch, N_HEADS),
        in_specs=[spec(Z_Q), spec(Z_K), spec(Z_V),
                  pl.BlockSpec((chunk, cols), lambda b, h: (step(b, h) // steps_per_chunk, 0)),
                  row_slab(cast_a), row_slab(cast_b)],
        out_specs=[pl.BlockSpec((seq, HEAD_DIM), lambda b, h: (b, h)),
                   pl.BlockSpec((1, cols, chunk), lambda b, h: (step(b, h) // steps_per_chunk, 0, 0)),
                   row_slab(cast_a), row_slab(cast_b)],
        out_shape=[jax.ShapeDtypeStruct((batch * seq, ATTN_DIM), BF16),
                   jax.ShapeDtypeStruct((rows // chunk, cols, chunk), BF16),
                   jax.ShapeDtypeStruct(cast_a.shape, BF16),
                   jax.ShapeDtypeStruct(cast_b.shape, BF16)],
        compiler_params=_params("parallel", "parallel"),
        name="moba_attention",
    )(z, z, z, table, cast_a, cast_b)


PEER_TOKEN_TILE = 512
PEER_GROUPS_PER_STEP = 4


def _merge_kernel(bz_ref, at_ref, wc_ref, wa_ref, gc_ref, ga_ref, bc_ref, ba_ref, o_ref,
                  wc_bf, wa_bf):
    _cast_at_first_row_tile([(wc_ref, wc_bf), (wa_ref, wa_bf)])
    y_conv = jnp.dot(bz_ref[...], wc_bf[...], preferred_element_type=F32)
    y_attn = jnp.dot(at_ref[...], wa_bf[...], preferred_element_type=F32)
    merged = (jax.nn.sigmoid(gc_ref[...] + bc_ref[...]) * y_conv
              + jax.nn.sigmoid(ga_ref[...] + ba_ref[...]) * y_attn)
    o_ref[...] = merged.astype(o_ref.dtype)


def _merge(bz, attn, w_conv_out, w_attn_out, z, b_gate, tm=1024, tn=512):
    m = bz.shape[0]
    nb = D_MODEL // tn
    bias = b_gate.reshape(1, 2 * D_MODEL)
    return pl.pallas_call(
        _merge_kernel,
        grid=(nb, m // tm),
        in_specs=[pl.BlockSpec((tm, CONV_DIM), lambda j, i: (i, 0)),
                  pl.BlockSpec((tm, ATTN_DIM), lambda j, i: (i, 0)),
                  pl.BlockSpec((CONV_DIM, tn), lambda j, i: (0, j)),
                  pl.BlockSpec((ATTN_DIM, tn), lambda j, i: (0, j)),
                  pl.BlockSpec((tm, tn), lambda j, i: (i, Z_GC // tn + j)),
                  pl.BlockSpec((tm, tn), lambda j, i: (i, Z_GA // tn + j)),
                  pl.BlockSpec((1, tn), lambda j, i: (0, j)),
                  pl.BlockSpec((1, tn), lambda j, i: (0, nb + j))],
        out_specs=pl.BlockSpec((tm, tn), lambda j, i: (i, j)),
        out_shape=jax.ShapeDtypeStruct((m, D_MODEL), BF16),
        scratch_shapes=[pltpu.VMEM((CONV_DIM, tn), BF16), pltpu.VMEM((ATTN_DIM, tn), BF16)],
        compiler_params=_params("parallel", "arbitrary"),
        name="merge",
    )(bz, attn, w_conv_out, w_attn_out, z, z, bias, bias)


def _sort16_pairs():
    pairs = []

    def merge(lo, n, r):
        step = r * 2
        if step < n:
            merge(lo, n, step)
            merge(lo + r, n, step)
            pairs.extend((i, i + r) for i in range(lo + r, lo + n - r, step))
        else:
            pairs.append((lo, lo + r))

    def sort(lo, n):
        if n > 1:
            sort(lo, n // 2)
            sort(lo + n // 2, n // 2)
            merge(lo, n, 1)

    sort(0, PEER_TOPK)
    return pairs


_SORT16 = _sort16_pairs()


def _compare_exchange(vals, i, j):
    a, b = vals[i], vals[j]
    if b is None:
        return
    if a is None:
        vals[i], vals[j] = b, None
        return
    vals[i], vals[j] = jnp.maximum(a, b), jnp.minimum(a, b)


def _top16(blocks):
    vals = list(blocks) + [None] * (PEER_TOPK - len(blocks))
    for i, j in _SORT16:
        _compare_exchange(vals, i, j)
    for shift in (4, 2, 1):
        other = [None if v is None else pltpu.roll(v, shift, axis=0) for v in vals]
        merged = []
        for i in range(PEER_TOPK):
            a, b = vals[i], other[PEER_TOPK - 1 - i]
            merged.append(b if a is None else a if b is None else jnp.maximum(a, b))
        vals = merged
        for step in (8, 4, 2, 1):
            for i in range(PEER_TOPK):
                if not i & step:
                    _compare_exchange(vals, i, i + step)
    return vals


def _sublane_sum(x):
    for shift in (4, 2, 1):
        x = x + pltpu.roll(x, shift, axis=0)
    return x


def _route_kernel(q_ref, sk_ref, n1_ref, rank2_ref, e1_ref, e2_ref):
    q = q_ref[...]
    tm = q.shape[0]
    dims = (((1,), (1,)), ((), ()))
    s1 = lax.dot_general(sk_ref[0, 0], q[:, :PEER_HALF], dims,
                         precision=lax.Precision.HIGHEST, preferred_element_type=F32)
    s2 = lax.dot_general(sk_ref[0, 1], q[:, PEER_HALF:], dims,
                         precision=lax.Precision.HIGHEST, preferred_element_type=F32)
    n_blocks = N_KEYS // SUBLANES
    s1_blocks = [s1[k * SUBLANES:(k + 1) * SUBLANES, :] for k in range(n_blocks)]
    s2_blocks = [s2[k * SUBLANES:(k + 1) * SUBLANES, :] for k in range(n_blocks)]
    t1 = _top16(s1_blocks)
    t2 = _top16(s2_blocks)

    sub = lax.broadcasted_iota(jnp.int32, (SUBLANES, tm), 0)
    spread = lambda rows: functools.reduce(
        lambda acc, b: jnp.where(sub == b, rows[b], acc), range(SUBLANES - 1), rows[SUBLANES - 1])
    t2_lo, t2_hi, t1_hi = spread(t2[:SUBLANES]), spread(t2[SUBLANES:]), spread(t1[SUBLANES:])
    cand = [t1[0] + t2_lo, t1[0] + t2_hi]
    for a in range(1, SUBLANES):
        n_valid = PEER_TOPK // (a + 1)
        piece = t1[a] + t2_lo
        cand.append(piece if n_valid >= SUBLANES else jnp.where(sub < n_valid, piece, NEG_INF))
    cand.append(t1_hi + t2[0])
    best = _top16(cand)
    tau = best[PEER_TOPK - 1]
    z = jnp.ones_like(tau)
    for r in range(1, PEER_TOPK):
        z = z + jnp.exp(best[r] - best[0])
    inv_z = 1.0 / z

    count = [_sublane_sum(jnp.where(t1[a] + t2_lo >= tau, 1.0, 0.0)
                          + jnp.where(t1[a] + t2_hi >= tau, 1.0, 0.0)) for a in range(PEER_TOPK)]
    n1, rank2, e1, e2 = [], [], [], []
    for k in range(n_blocks):
        n1_k = jnp.zeros((SUBLANES, tm), F32)
        rank2_k = jnp.full((SUBLANES, tm), float(PEER_TOPK), F32)
        for a in reversed(range(PEER_TOPK)):
            n1_k = jnp.where(s1_blocks[k] == t1[a], count[a], n1_k)
            rank2_k = jnp.where(s2_blocks[k] == t2[a], float(a), rank2_k)
        n1.append(n1_k)
        rank2.append(rank2_k)
        e1.append(jnp.exp(s1_blocks[k] - t1[0]) * inv_z)
        e2.append(jnp.exp(s2_blocks[k] - t2[0]))
    n1_ref[0, 0] = jnp.concatenate(n1, axis=0)
    rank2_ref[0, 0] = jnp.concatenate(rank2, axis=0).astype(rank2_ref.dtype)
    e1_ref[0, 0] = jnp.concatenate(e1, axis=0)
    e2_ref[0, 0] = jnp.concatenate(e2, axis=0).astype(e2_ref.dtype)


def _peer_route(qp, sub_keys, tm):
    t = qp.shape[0]
    key_out = pl.BlockSpec((1, 1, N_KEYS, tm), lambda i, h: (i, h, 0, 0))
    key_shape = lambda dtype: jax.ShapeDtypeStruct((t // tm, PEER_HEADS, N_KEYS, tm), dtype)
    return pl.pallas_call(
        _route_kernel,
        grid=(t // tm, PEER_HEADS),
        in_specs=[pl.BlockSpec((tm, 2 * PEER_HALF), lambda i, h: (i, h)),
                  pl.BlockSpec((1, 2, N_KEYS, PEER_HALF), lambda i, h: (h, 0, 0, 0))],
        out_specs=[key_out, key_out, key_out, key_out],
        out_shape=[key_shape(F32), key_shape(BF16), key_shape(F32), key_shape(BF16)],
        compiler_params=_params("parallel", "parallel"),
        name="peer_route",
    )(qp, sub_keys)


def _peer_kernel(xt_ref, u_ref, vt_ref, n1_ref, rank2_ref, e1_ref, e2_ref, o_ref, *, n_grp):
    c = pl.program_id(1)

    @pl.when(c == 0)
    def _():
        o_ref[...] = jnp.zeros_like(o_ref)

    hid = jnp.dot(u_ref[...], xt_ref[0], preferred_element_type=F32)
    sqrt_half = 0.7071067811865476
    tt = hid.shape[1]
    pack = 2 * SUBLANES
    w_parts = []
    for g in range(n_grp):
        i = c * n_grp + g
        spread = lambda ref, h: jnp.broadcast_to(ref[0, h, pl.ds(i, 1), :], (pack, tt)).astype(BF16)
        n1 = [spread(n1_ref, h) for h in range(PEER_HEADS)]
        e1 = [spread(e1_ref, h) for h in range(PEER_HEADS)]
        for lo in range(0, N_KEYS, pack):
            keys = slice(lo, lo + pack)
            gate = None
            for h in range(PEER_HEADS):
                contrib = jnp.where(rank2_ref[0, h, keys, :] < n1[h], e1[h] * e2_ref[0, h, keys, :],
                                    jnp.zeros((), BF16))
                gate = contrib if gate is None else gate + contrib
            hg = hid[g * N_KEYS + lo:g * N_KEYS + lo + pack, :]
            act = 0.5 * hg * (1.0 + lax.erf(hg * sqrt_half))
            w_parts.append(act.astype(BF16) * gate)
    w = jnp.concatenate(w_parts, axis=0)
    o_ref[0] += jnp.dot(vt_ref[0], w, preferred_element_type=F32)


def _peer_mix(xnt, u, vt, n1, rank2, e1, e2):
    n_tiles, d, tt = xnt.shape
    n_chunks, _, ec = vt.shape
    n_grp = ec // N_KEYS
    once = pl.Buffered(1)
    tok = pl.BlockSpec((1, PEER_HEADS, N_KEYS, tt), lambda i, c: (i, 0, 0, 0), pipeline_mode=once)
    return pl.pallas_call(
        functools.partial(_peer_kernel, n_grp=n_grp),
        grid=(n_tiles, n_chunks),
        in_specs=[pl.BlockSpec((1, d, tt), lambda i, c: (i, 0, 0), pipeline_mode=once),
                  pl.BlockSpec((ec, d), lambda i, c: (c, 0)),
                  pl.BlockSpec((1, d, ec), lambda i, c: (c, 0, 0)),
                  tok, tok, tok, tok],
        out_specs=pl.BlockSpec((1, d, tt), lambda i, c: (i, 0, 0)),
        out_shape=jax.ShapeDtypeStruct((n_tiles, d, tt), F32),
        compiler_params=_params("parallel", "arbitrary"),
        name="peer_mix",
    )(xnt, u, vt, n1, rank2, e1, e2)


def kernel(x, norm_mix, w_in, b_gate, conv_w, w_conv_out, w_attn_out, w_o, norm_ffn,
           w_peer_q, sub_keys, u_emb, v_emb, norm_final):
    batch, seq, d = x.shape
    assert w_in.shape[0] == 1, "single-layer block"
    h = x.reshape(batch * seq, d)
    hn = _rmsnorm(h, norm_mix[0], BF16)
    bz = _conv_proj(hn, w_in[0], conv_w[0], seq)
    z = _matmul(hn, w_in[0], F32, col0=COL_Q, n=IN_COLS - COL_Q)
    attn, v_t, u_bf, w_o_bf = _moba_attention(z, v_emb[0], PEER_GROUPS_PER_STEP * N_KEYS,
                                               u_emb[0], w_o[0], batch, seq)
    merged = _merge(bz, attn, w_conv_out[0], w_attn_out[0], z, b_gate[0])
    h = _matmul_residual(merged, w_o_bf, h)
    hn, hnt = _rmsnorm_with_transpose(h, norm_ffn[0], PEER_TOKEN_TILE)
    qp = _matmul(hn, w_peer_q[0], F32, col0=0, n=w_peer_q.shape[-1])
    n1, rank2, e1, e2 = _peer_route(qp, sub_keys[0], PEER_TOKEN_TILE)
    peer_t = _peer_mix(hnt, u_bf, v_t, n1, rank2, e1, e2)
    return _add_rmsnorm(h, peer_t, norm_final).reshape(batch, seq, d)
```

```python
import functools

import jax
import jax.numpy as jnp
from jax import lax
from jax.experimental import pallas as pl
from jax.experimental.pallas import tpu as pltpu

D_MODEL = 4096
CONV_DIM = 2048
CONV_WIDTH = 3
N_HEADS = 16
HEAD_DIM = 128
ATTN_DIM = N_HEADS * HEAD_DIM
MOBA_BLOCK = 256
MOBA_TOPK = 3
PEER_HEADS = 8
N_KEYS = 128
N_EXPERTS = N_KEYS * N_KEYS
PEER_HALF = 128
PEER_TOPK = 16
EPS = 1e-6

COL_B = 0
COL_C = CONV_DIM
COL_U = 2 * CONV_DIM
COL_Q = 3 * CONV_DIM
COL_K = COL_Q + ATTN_DIM
COL_V = COL_K + ATTN_DIM
COL_GC = COL_V + ATTN_DIM
COL_GA = COL_GC + D_MODEL
IN_COLS = COL_GA + D_MODEL
Z_Q = 0
Z_K = COL_K - COL_Q
Z_V = COL_V - COL_Q
Z_GC = COL_GC - COL_Q
Z_GA = COL_GA - COL_Q

VMEM_LIMIT_BYTES = 56 * 1024 * 1024

F32 = jnp.float32
BF16 = jnp.bfloat16
NEG_INF = float("-inf")
SUBLANES = 8


def _params(*semantics):
    return pltpu.CompilerParams(dimension_semantics=semantics,
                                vmem_limit_bytes=VMEM_LIMIT_BYTES)


def _rmsnorm_kernel(x_ref, g_ref, o_ref):
    x = x_ref[...]
    y = x * lax.rsqrt(jnp.mean(x * x, axis=-1, keepdims=True) + EPS)
    o_ref[...] = (y * g_ref[...]).astype(o_ref.dtype)


def _rmsnorm(x, g, out_dtype, tm=256):
    m, d = x.shape
    return pl.pallas_call(
        _rmsnorm_kernel,
        grid=(m // tm,),
        in_specs=[pl.BlockSpec((tm, d), lambda i: (i, 0)),
                  pl.BlockSpec((1, d), lambda i: (0, 0))],
        out_specs=pl.BlockSpec((tm, d), lambda i: (i, 0)),
        out_shape=jax.ShapeDtypeStruct((m, d), out_dtype),
        compiler_params=_params("parallel"),
        name="rmsnorm",
    )(x, g.reshape(1, d))


def _rmsnorm_t_kernel(x_ref, g_ref, o_ref, ot_ref):
    x = x_ref[...]
    y = x * lax.rsqrt(jnp.mean(x * x, axis=-1, keepdims=True) + EPS) * g_ref[...]
    o_ref[...] = y.astype(o_ref.dtype)
    ot_ref[...] = y.T.astype(ot_ref.dtype)


def _rmsnorm_with_transpose(x, g, tm=256):
    m, d = x.shape
    return pl.pallas_call(
        _rmsnorm_t_kernel,
        grid=(m // tm,),
        in_specs=[pl.BlockSpec((tm, d), lambda i: (i, 0)),
                  pl.BlockSpec((1, d), lambda i: (0, 0))],
        out_specs=[pl.BlockSpec((tm, d), lambda i: (i, 0)),
                   pl.BlockSpec((d, tm), lambda i: (0, i))],
        out_shape=[jax.ShapeDtypeStruct((m, d), BF16),
                   jax.ShapeDtypeStruct((d, m), BF16)],
        compiler_params=_params("parallel"),
        name="rmsnorm_t",
    )(x, g.reshape(1, d))


def _add_rmsnorm_kernel(a_ref, bt_ref, g_ref, o_ref):
    x = a_ref[...] + bt_ref[...].T
    y = x * lax.rsqrt(jnp.mean(x * x, axis=-1, keepdims=True) + EPS)
    o_ref[...] = (y * g_ref[...]).astype(o_ref.dtype)


def _add_rmsnorm(a, bt, g, tm=256):
    m, d = a.shape
    return pl.pallas_call(
        _add_rmsnorm_kernel,
        grid=(m // tm,),
        in_specs=[pl.BlockSpec((tm, d), lambda i: (i, 0)),
                  pl.BlockSpec((d, tm), lambda i: (0, i)),
                  pl.BlockSpec((1, d), lambda i: (0, 0))],
        out_specs=pl.BlockSpec((tm, d), lambda i: (i, 0)),
        out_shape=jax.ShapeDtypeStruct((m, d), F32),
        compiler_params=_params("parallel"),
        name="add_rmsnorm",
    )(a, bt, g.reshape(1, d))


def _cast_at_first_row_tile(pairs):
    @pl.when(pl.program_id(1) == 0)
    def _():
        for src, dst in pairs:
            dst[...] = src[...].astype(dst.dtype)


def _matmul_kernel(a_ref, w_ref, o_ref, wb_ref):
    _cast_at_first_row_tile([(w_ref, wb_ref)])
    o_ref[...] = jnp.dot(a_ref[...], wb_ref[...],
                         preferred_element_type=F32).astype(o_ref.dtype)


def _matmul(a, w, out_dtype, col0, n, tm=512, tn=1024):
    m, k = a.shape
    return pl.pallas_call(
        _matmul_kernel,
        grid=(n // tn, m // tm),
        in_specs=[pl.BlockSpec((tm, k), lambda j, i: (i, 0)),
                  pl.BlockSpec((k, tn), lambda j, i: (0, col0 // tn + j))],
        out_specs=pl.BlockSpec((tm, tn), lambda j, i: (i, j)),
        out_shape=jax.ShapeDtypeStruct((m, n), out_dtype),
        scratch_shapes=[pltpu.VMEM((k, tn), BF16)],
        compiler_params=_params("parallel", "arbitrary"),
        name="matmul",
    )(a, w)


def _matmul_residual_kernel(a_ref, w_ref, r_ref, o_ref):
    o_ref[...] = r_ref[...] + jnp.dot(a_ref[...], w_ref[...], preferred_element_type=F32)


def _matmul_residual(a, w, r, tm=1024, tn=1024):
    m, k = a.shape
    _, n = w.shape
    return pl.pallas_call(
        _matmul_residual_kernel,
        grid=(n // tn, m // tm),
        in_specs=[pl.BlockSpec((tm, k), lambda j, i: (i, 0)),
                  pl.BlockSpec((k, tn), lambda j, i: (0, j)),
                  pl.BlockSpec((tm, tn), lambda j, i: (i, j))],
        out_specs=pl.BlockSpec((tm, tn), lambda j, i: (i, j)),
        out_shape=jax.ShapeDtypeStruct((m, n), F32),
        compiler_params=_params("parallel", "parallel"),
        name="matmul_residual",
    )(a, w, r)


def _conv_proj_kernel(a_ref, wb_ref, wc_ref, wu_ref, cw_ref, o_ref,
                      wb_bf, wc_bf, wu_bf, hist_ref, *, tiles_per_seq):
    _cast_at_first_row_tile([(wb_ref, wb_bf), (wc_ref, wc_bf), (wu_ref, wu_bf)])

    @pl.when(pl.program_id(1) % tiles_per_seq == 0)
    def _():
        hist_ref[...] = jnp.zeros_like(hist_ref)

    a = a_ref[...]
    z = (jnp.dot(a, wc_bf[...], preferred_element_type=F32)
         * jnp.dot(a, wu_bf[...], preferred_element_type=F32))
    hist = hist_ref[...]
    n_hist = hist.shape[0]
    row = lax.broadcasted_iota(jnp.int32, hist.shape, 0)
    w = cw_ref[...]
    acc = z * w[2:3, :]
    for shift in (1, 2):
        zs = pltpu.roll(z, shift, axis=0)
        top = jnp.where(row >= shift, zs[:n_hist], pltpu.roll(hist, shift, axis=0))
        zs = jnp.concatenate([top, zs[n_hist:]], axis=0)
        acc = acc + zs * w[2 - shift:3 - shift, :]
    hist_ref[...] = z[z.shape[0] - n_hist:]
    o_ref[...] = (jnp.dot(a, wb_bf[...], preferred_element_type=F32) * acc).astype(o_ref.dtype)


def _conv_proj(a, w_in, conv_w, seq, tm=1024, tn=256):
    m, k = a.shape
    nb = CONV_DIM // tn
    wspec = lambda off: pl.BlockSpec((k, tn), lambda j, i: (0, off // tn + j))
    return pl.pallas_call(
        functools.partial(_conv_proj_kernel, tiles_per_seq=seq // tm),
        grid=(nb, m // tm),
        in_specs=[pl.BlockSpec((tm, k), lambda j, i: (i, 0)),
                  wspec(COL_B), wspec(COL_C), wspec(COL_U),
                  pl.BlockSpec((CONV_WIDTH, tn), lambda j, i: (0, j))],
        out_specs=pl.BlockSpec((tm, tn), lambda j, i: (i, j)),
        out_shape=jax.ShapeDtypeStruct((m, CONV_DIM), BF16),
        scratch_shapes=[pltpu.VMEM((k, tn), BF16)] * 3 + [pltpu.VMEM((SUBLANES, tn), F32)],
        compiler_params=_params("parallel", "arbitrary"),
        name="conv_proj",
    )(a, w_in, w_in, w_in, conv_w)


def _moba_kernel(q_ref, k_ref, v_ref, tab_ref, ca_ref, cb_ref, o_ref, tab_t_ref, ca_bf_ref, cb_bf_ref,
                 *, n_blk):
    tab_t_ref[...] = tab_ref[...].T.astype(tab_t_ref.dtype)
    ca_bf_ref[...] = ca_ref[...].astype(ca_bf_ref.dtype)
    cb_bf_ref[...] = cb_ref[...].astype(cb_bf_ref.dtype)
    seq = q_ref.shape[0]
    nt_dims = (((1,), (1,)), ((), ()))
    k = k_ref[...]
    kb = k.astype(BF16)
    vb = v_ref[...].astype(BF16)
    q_all = q_ref[...]
    k_mean = jnp.mean(k.reshape(n_blk, MOBA_BLOCK, HEAD_DIM), axis=1)

    gate = lax.dot_general(k_mean, q_all, nt_dims, precision=lax.Precision.HIGHEST,
                           preferred_element_type=F32)
    blk = lax.broadcasted_iota(jnp.int32, (n_blk, seq), 0)
    own = lax.broadcasted_iota(jnp.int32, (n_blk, seq), 1) // MOBA_BLOCK
    past = blk < own
    gate = jnp.where(past, gate, NEG_INF)
    rank = jnp.zeros((n_blk, seq), jnp.int32)
    for jp in range(n_blk - 1):
        gj = gate[jp:jp + 1, :]
        ahead = (gj > gate) | ((gj == gate) & (blk > jp))
        rank = rank + ahead.astype(jnp.int32)
    sel = jnp.where((rank < MOBA_TOPK) & past, 1.0, 0.0).astype(BF16)
    expand = jnp.where(lax.broadcasted_iota(jnp.int32, (n_blk, seq), 1) // MOBA_BLOCK == blk,
                       1.0, 0.0).astype(BF16)

    row = lax.broadcasted_iota(jnp.int32, (MOBA_BLOCK, MOBA_BLOCK), 0)
    col = lax.broadcasted_iota(jnp.int32, (MOBA_BLOCK, MOBA_BLOCK), 1)
    causal = col <= row
    q_scale = HEAD_DIM ** -0.5 * 1.4426950408889634
    for i in range(n_blk):
        lo, nk = i * MOBA_BLOCK, (i + 1) * MOBA_BLOCK
        q = (q_all[lo:nk] * q_scale).astype(BF16)
        s = lax.dot_general(q, kb[:nk], nt_dims, preferred_element_type=F32)
        s_own = jnp.where(causal, s[:, lo:nk], NEG_INF)
        if i > 0:
            keep = lax.dot_general(sel[:, lo:nk], expand[:, :lo], (((0,), (0,)), ((), ())),
                                   preferred_element_type=F32)
            s = jnp.concatenate([jnp.where(keep > 0.5, s[:, :lo], NEG_INF), s_own], axis=1)
        else:
            s = s_own
        m = jnp.max(s, axis=-1, keepdims=True)
        p = jnp.exp2(s - m)
        l = jnp.sum(p, axis=-1, keepdims=True)
        out = jnp.dot(p.astype(BF16), vb[:nk], preferred_element_type=F32) / l
        o_ref[lo:nk, :] = out.astype(o_ref.dtype)


def _moba_attention(z, table, cast_a, cast_b, batch, seq):
    n_blk = seq // MOBA_BLOCK
    steps = batch * N_HEADS
    step = lambda b, h: b * N_HEADS + h
    slab = lambda x: x.shape[0] // steps
    row_slab = lambda x: pl.BlockSpec((slab(x), x.shape[1]), lambda b, h: (step(b, h), 0))
    spec = lambda off: pl.BlockSpec((seq, HEAD_DIM), lambda b, h: (b, off // HEAD_DIM + h))
    rows, cols = table.shape
    return pl.pallas_call(
        functools.partial(_moba_kernel, n_blk=n_blk),
        grid=(batch, N_HEADS),
        in_specs=[spec(Z_Q), spec(Z_K), spec(Z_V), row_slab(table), row_slab(cast_a), row_slab(cast_b)],
        out_specs=[pl.BlockSpec((seq, HEAD_DIM), lambda b, h: (b, h)),
                   pl.BlockSpec((cols, slab(table)), lambda b, h: (0, step(b, h))),
                   row_slab(cast_a), row_slab(cast_b)],
        out_shape=[jax.ShapeDtypeStruct((batch * seq, ATTN_DIM), BF16),
                   jax.ShapeDtypeStruct((cols, rows), BF16),
                   jax.ShapeDtypeStruct(cast_a.shape, BF16),
                   jax.ShapeDtypeStruct(cast_b.shape, BF16)],
        compiler_params=_params("parallel", "parallel"),
        name="moba_attention",
    )(z, z, z, table, cast_a, cast_b)


def _merge_kernel(bz_ref, at_ref, wc_ref, wa_ref, gc_ref, ga_ref, bc_ref, ba_ref, o_ref,
                  wc_bf, wa_bf):
    _cast_at_first_row_tile([(wc_ref, wc_bf), (wa_ref, wa_bf)])
    y_conv = jnp.dot(bz_ref[...], wc_bf[...], preferred_element_type=F32)
    y_attn = jnp.dot(at_ref[...], wa_bf[...], preferred_element_type=F32)
    merged = (jax.nn.sigmoid(gc_ref[...] + bc_ref[...]) * y_conv
              + jax.nn.sigmoid(ga_ref[...] + ba_ref[...]) * y_attn)
    o_ref[...] = merged.astype(o_ref.dtype)


def _merge(bz, attn, w_conv_out, w_attn_out, z, b_gate, tm=1024, tn=512):
    m = bz.shape[0]
    nb = D_MODEL // tn
    bias = b_gate.reshape(1, 2 * D_MODEL)
    return pl.pallas_call(
        _merge_kernel,
        grid=(nb, m // tm),
        in_specs=[pl.BlockSpec((tm, CONV_DIM), lambda j, i: (i, 0)),
                  pl.BlockSpec((tm, ATTN_DIM), lambda j, i: (i, 0)),
                  pl.BlockSpec((CONV_DIM, tn), lambda j, i: (0, j)),
                  pl.BlockSpec((ATTN_DIM, tn), lambda j, i: (0, j)),
                  pl.BlockSpec((tm, tn), lambda j, i: (i, Z_GC // tn + j)),
                  pl.BlockSpec((tm, tn), lambda j, i: (i, Z_GA // tn + j)),
                  pl.BlockSpec((1, tn), lambda j, i: (0, j)),
                  pl.BlockSpec((1, tn), lambda j, i: (0, nb + j))],
        out_specs=pl.BlockSpec((tm, tn), lambda j, i: (i, j)),
        out_shape=jax.ShapeDtypeStruct((m, D_MODEL), BF16),
        scratch_shapes=[pltpu.VMEM((CONV_DIM, tn), BF16), pltpu.VMEM((ATTN_DIM, tn), BF16)],
        compiler_params=_params("parallel", "arbitrary"),
        name="merge",
    )(bz, attn, w_conv_out, w_attn_out, z, z, bias, bias)


def _sort16_pairs():
    pairs = []

    def merge(lo, n, r):
        step = r * 2
        if step < n:
            merge(lo, n, step)
            merge(lo + r, n, step)
            pairs.extend((i, i + r) for i in range(lo + r, lo + n - r, step))
        else:
            pairs.append((lo, lo + r))

    def sort(lo, n):
        if n > 1:
            sort(lo, n // 2)
            sort(lo + n // 2, n // 2)
            merge(lo, n, 1)

    sort(0, PEER_TOPK)
    return pairs


_SORT16 = _sort16_pairs()


def _compare_exchange(vals, i, j):
    a, b = vals[i], vals[j]
    if b is None:
        return
    if a is None:
        vals[i], vals[j] = b, None
        return
    vals[i], vals[j] = jnp.maximum(a, b), jnp.minimum(a, b)


def _top16(blocks):
    vals = list(blocks) + [None] * (PEER_TOPK - len(blocks))
    for i, j in _SORT16:
        _compare_exchange(vals, i, j)
    for shift in (4, 2, 1):
        other = [None if v is None else pltpu.roll(v, shift, axis=0) for v in vals]
        merged = []
        for i in range(PEER_TOPK):
            a, b = vals[i], other[PEER_TOPK - 1 - i]
            merged.append(b if a is None else a if b is None else jnp.maximum(a, b))
        vals = merged
        for step in (8, 4, 2, 1):
            for i in range(PEER_TOPK):
                if not i & step:
                    _compare_exchange(vals, i, i + step)
    return vals


def _sublane_sum(x):
    for shift in (4, 2, 1):
        x = x + pltpu.roll(x, shift, axis=0)
    return x


def _route_kernel(a_ref, w_ref, sk_ref, n1_ref, rank2_ref, e1_ref, e2_ref, wb_ref):
    _cast_at_first_row_tile([(w_ref, wb_ref)])
    q = jnp.dot(a_ref[...], wb_ref[...], preferred_element_type=F32)
    tm = q.shape[0]
    dims = (((1,), (1,)), ((), ()))
    s1 = lax.dot_general(sk_ref[0, 0], q[:, :PEER_HALF], dims,
                         precision=lax.Precision.HIGHEST, preferred_element_type=F32)
    s2 = lax.dot_general(sk_ref[0, 1], q[:, PEER_HALF:], dims,
                         precision=lax.Precision.HIGHEST, preferred_element_type=F32)
    n_blocks = N_KEYS // SUBLANES
    s1_blocks = [s1[k * SUBLANES:(k + 1) * SUBLANES, :] for k in range(n_blocks)]
    s2_blocks = [s2[k * SUBLANES:(k + 1) * SUBLANES, :] for k in range(n_blocks)]
    t1 = _top16(s1_blocks)
    t2 = _top16(s2_blocks)

    sub = lax.broadcasted_iota(jnp.int32, (SUBLANES, tm), 0)
    spread = lambda rows: functools.reduce(
        lambda acc, b: jnp.where(sub == b, rows[b], acc), range(SUBLANES - 1), rows[SUBLANES - 1])
    t2_lo, t2_hi, t1_hi = spread(t2[:SUBLANES]), spread(t2[SUBLANES:]), spread(t1[SUBLANES:])
    cand = [t1[0] + t2_lo, t1[0] + t2_hi]
    for a in range(1, SUBLANES):
        n_valid = PEER_TOPK // (a + 1)
        piece = t1[a] + t2_lo
        cand.append(piece if n_valid >= SUBLANES else jnp.where(sub < n_valid, piece, NEG_INF))
    cand.append(t1_hi + t2[0])
    best = _top16(cand)
    tau = best[PEER_TOPK - 1]
    z = jnp.ones_like(tau)
    for r in range(1, PEER_TOPK):
        z = z + jnp.exp(best[r] - best[0])
    inv_z = 1.0 / z

    count = [_sublane_sum(jnp.where(t1[a] + t2_lo >= tau, 1.0, 0.0)
                          + jnp.where(t1[a] + t2_hi >= tau, 1.0, 0.0)) for a in range(PEER_TOPK)]
    n1, rank2, e1, e2 = [], [], [], []
    for k in range(n_blocks):
        n1_k = jnp.zeros((SUBLANES, tm), F32)
        rank2_k = jnp.full((SUBLANES, tm), float(PEER_TOPK), F32)
        for a in reversed(range(PEER_TOPK)):
            n1_k = jnp.where(s1_blocks[k] == t1[a], count[a], n1_k)
            rank2_k = jnp.where(s2_blocks[k] == t2[a], float(a), rank2_k)
        n1.append(n1_k)
        rank2.append(rank2_k)
        e1.append(jnp.exp(s1_blocks[k] - t1[0]) * inv_z)
        e2.append(jnp.exp(s2_blocks[k] - t2[0]))
    n1_ref[0] = jnp.concatenate(n1, axis=0)
    rank2_ref[0] = jnp.concatenate(rank2, axis=0).astype(rank2_ref.dtype)
    e1_ref[0] = jnp.concatenate(e1, axis=0)
    e2_ref[0] = jnp.concatenate(e2, axis=0).astype(e2_ref.dtype)


def _peer_route(a, w_q, sub_keys, tm=512):
    t, k = a.shape
    key_out = pl.BlockSpec((1, N_KEYS, tm), lambda h, i: (h, 0, i))
    key_shape = lambda dtype: jax.ShapeDtypeStruct((PEER_HEADS, N_KEYS, t), dtype)
    return pl.pallas_call(
        _route_kernel,
        grid=(PEER_HEADS, t // tm),
        in_specs=[pl.BlockSpec((tm, k), lambda h, i: (i, 0)),
                  pl.BlockSpec((k, 2 * PEER_HALF), lambda h, i: (0, h)),
                  pl.BlockSpec((1, 2, N_KEYS, PEER_HALF), lambda h, i: (h, 0, 0, 0))],
        out_specs=[key_out, key_out, key_out, key_out],
        out_shape=[key_shape(F32), key_shape(BF16), key_shape(F32), key_shape(BF16)],
        scratch_shapes=[pltpu.VMEM((k, 2 * PEER_HALF), BF16)],
        compiler_params=_params("parallel", "arbitrary"),
        name="peer_route",
    )(a, w_q, sub_keys)


def _peer_kernel(xt_ref, u_ref, vt_ref, n1_ref, rank2_ref, e1_ref, e2_ref, o_ref, *, n_grp):
    c = pl.program_id(1)

    @pl.when(c == 0)
    def _():
        o_ref[...] = jnp.zeros_like(o_ref)

    hid = jnp.dot(u_ref[...], xt_ref[...], preferred_element_type=F32)
    sqrt_half = 0.7071067811865476
    tt = hid.shape[1]
    pack = 2 * SUBLANES
    w_parts = []
    for g in range(n_grp):
        i = c * n_grp + g
        spread = lambda ref, h: jnp.broadcast_to(ref[h, pl.ds(i, 1), :], (pack, tt)).astype(BF16)
        n1 = [spread(n1_ref, h) for h in range(PEER_HEADS)]
        e1 = [spread(e1_ref, h) for h in range(PEER_HEADS)]
        for lo in range(0, N_KEYS, pack):
            keys = slice(lo, lo + pack)
            gate = None
            for h in range(PEER_HEADS):
                contrib = jnp.where(rank2_ref[h, keys, :] < n1[h], e1[h] * e2_ref[h, keys, :],
                                    jnp.zeros((), BF16))
                gate = contrib if gate is None else gate + contrib
            hg = hid[g * N_KEYS + lo:g * N_KEYS + lo + pack, :]
            act = 0.5 * hg * (1.0 + lax.erf(hg * sqrt_half))
            w_parts.append(act.astype(BF16) * gate)
    w = jnp.concatenate(w_parts, axis=0)
    o_ref[...] += jnp.dot(vt_ref[...], w, preferred_element_type=F32)


def _peer_mix(xnt, u, vt, n1, rank2, e1, e2, tt=512, n_grp=4):
    d, t = xnt.shape
    ec = n_grp * N_KEYS
    once = pl.Buffered(1)
    tok = pl.BlockSpec((PEER_HEADS, N_KEYS, tt), lambda i, c: (0, 0, i), pipeline_mode=once)
    return pl.pallas_call(
        functools.partial(_peer_kernel, n_grp=n_grp),
        grid=(t // tt, N_EXPERTS // ec),
        in_specs=[pl.BlockSpec((d, tt), lambda i, c: (0, i), pipeline_mode=once),
                  pl.BlockSpec((ec, d), lambda i, c: (c, 0)),
                  pl.BlockSpec((d, ec), lambda i, c: (0, c)),
                  tok, tok, tok, tok],
        out_specs=pl.BlockSpec((d, tt), lambda i, c: (0, i)),
        out_shape=jax.ShapeDtypeStruct((d, t), F32),
        compiler_params=_params("parallel", "arbitrary"),
        name="peer_mix",
    )(xnt, u, vt, n1, rank2, e1, e2)


def kernel(x, norm_mix, w_in, b_gate, conv_w, w_conv_out, w_attn_out, w_o, norm_ffn,
           w_peer_q, sub_keys, u_emb, v_emb, norm_final):
    batch, seq, d = x.shape
    assert w_in.shape[0] == 1, "single-layer block"
    h = x.reshape(batch * seq, d)
    hn = _rmsnorm(h, norm_mix[0], BF16)
    bz = _conv_proj(hn, w_in[0], conv_w[0], seq)
    z = _matmul(hn, w_in[0], F32, col0=COL_Q, n=IN_COLS - COL_Q)
    attn, v_t, u_bf, w_o_bf = _moba_attention(z, v_emb[0], u_emb[0], w_o[0], batch, seq)
    merged = _merge(bz, attn, w_conv_out[0], w_attn_out[0], z, b_gate[0])
    h = _matmul_residual(merged, w_o_bf, h)
    hn, hnt = _rmsnorm_with_transpose(h, norm_ffn[0])
    n1, rank2, e1, e2 = _peer_route(hn, w_peer_q[0], sub_keys[0])
    peer_t = _peer_mix(hnt, u_bf, v_t, n1, rank2, e1, e2)
    return _add_rmsnorm(h, peer_t, norm_final).reshape(batch, seq, d)
```

```python
import functools

import jax
import jax.numpy as jnp
from jax import lax
from jax.experimental import pallas as pl
from jax.experimental.pallas import tpu as pltpu

D_MODEL = 4096
CONV_DIM = 2048
CONV_WIDTH = 3
N_HEADS = 16
HEAD_DIM = 128
ATTN_DIM = N_HEADS * HEAD_DIM
MOBA_BLOCK = 256
MOBA_TOPK = 3
PEER_HEADS = 8
N_KEYS = 128
N_EXPERTS = N_KEYS * N_KEYS
PEER_HALF = 128
PEER_TOPK = 16
EPS = 1e-6

COL_B = 0
COL_C = CONV_DIM
COL_U = 2 * CONV_DIM
COL_Q = 3 * CONV_DIM
COL_K = COL_Q + ATTN_DIM
COL_V = COL_K + ATTN_DIM
COL_GC = COL_V + ATTN_DIM
COL_GA = COL_GC + D_MODEL
IN_COLS = COL_GA + D_MODEL
Z_Q = 0
Z_K = COL_K - COL_Q
Z_V = COL_V - COL_Q
Z_GC = COL_GC - COL_Q
Z_GA = COL_GA - COL_Q

VMEM_LIMIT_BYTES = 56 * 1024 * 1024

F32 = jnp.float32
BF16 = jnp.bfloat16
NEG_INF = float("-inf")
SUBLANES = 8


def _params(*semantics):
    return pltpu.CompilerParams(dimension_semantics=semantics,
                                vmem_limit_bytes=VMEM_LIMIT_BYTES)


def _rmsnorm_kernel(x_ref, g_ref, o_ref):
    x = x_ref[...]
    y = x * lax.rsqrt(jnp.mean(x * x, axis=-1, keepdims=True) + EPS)
    o_ref[...] = (y * g_ref[...]).astype(o_ref.dtype)


def _rmsnorm(x, g, out_dtype, tm=256):
    m, d = x.shape
    return pl.pallas_call(
        _rmsnorm_kernel,
        grid=(m // tm,),
        in_specs=[pl.BlockSpec((tm, d), lambda i: (i, 0)),
                  pl.BlockSpec((1, d), lambda i: (0, 0))],
        out_specs=pl.BlockSpec((tm, d), lambda i: (i, 0)),
        out_shape=jax.ShapeDtypeStruct((m, d), out_dtype),
        compiler_params=_params("parallel"),
        name="rmsnorm",
    )(x, g.reshape(1, d))


def _rmsnorm_t_kernel(x_ref, g_ref, o_ref, ot_ref):
    x = x_ref[...]
    y = x * lax.rsqrt(jnp.mean(x * x, axis=-1, keepdims=True) + EPS) * g_ref[...]
    o_ref[...] = y.astype(o_ref.dtype)
    ot_ref[...] = y.T.astype(ot_ref.dtype)


def _rmsnorm_with_transpose(x, g, tm=256):
    m, d = x.shape
    return pl.pallas_call(
        _rmsnorm_t_kernel,
        grid=(m // tm,),
        in_specs=[pl.BlockSpec((tm, d), lambda i: (i, 0)),
                  pl.BlockSpec((1, d), lambda i: (0, 0))],
        out_specs=[pl.BlockSpec((tm, d), lambda i: (i, 0)),
                   pl.BlockSpec((d, tm), lambda i: (0, i))],
        out_shape=[jax.ShapeDtypeStruct((m, d), BF16),
                   jax.ShapeDtypeStruct((d, m), BF16)],
        compiler_params=_params("parallel"),
        name="rmsnorm_t",
    )(x, g.reshape(1, d))


def _add_rmsnorm_kernel(a_ref, bt_ref, g_ref, o_ref):
    x = a_ref[...] + bt_ref[...].T
    y = x * lax.rsqrt(jnp.mean(x * x, axis=-1, keepdims=True) + EPS)
    o_ref[...] = (y * g_ref[...]).astype(o_ref.dtype)


def _add_rmsnorm(a, bt, g, tm=256):
    m, d = a.shape
    return pl.pallas_call(
        _add_rmsnorm_kernel,
        grid=(m // tm,),
        in_specs=[pl.BlockSpec((tm, d), lambda i: (i, 0)),
                  pl.BlockSpec((d, tm), lambda i: (0, i)),
                  pl.BlockSpec((1, d), lambda i: (0, 0))],
        out_specs=pl.BlockSpec((tm, d), lambda i: (i, 0)),
        out_shape=jax.ShapeDtypeStruct((m, d), F32),
        compiler_params=_params("parallel"),
        name="add_rmsnorm",
    )(a, bt, g.reshape(1, d))


def _cast_at_first_row_tile(pairs):
    @pl.when(pl.program_id(1) == 0)
    def _():
        for src, dst in pairs:
            dst[...] = src[...].astype(dst.dtype)


def _matmul_kernel(a_ref, w_ref, o_ref, wb_ref):
    _cast_at_first_row_tile([(w_ref, wb_ref)])
    o_ref[...] = jnp.dot(a_ref[...], wb_ref[...],
                         preferred_element_type=F32).astype(o_ref.dtype)


def _matmul(a, w, out_dtype, col0, n, tm=512, tn=1024):
    m, k = a.shape
    return pl.pallas_call(
        _matmul_kernel,
        grid=(n // tn, m // tm),
        in_specs=[pl.BlockSpec((tm, k), lambda j, i: (i, 0)),
                  pl.BlockSpec((k, tn), lambda j, i: (0, col0 // tn + j))],
        out_specs=pl.BlockSpec((tm, tn), lambda j, i: (i, j)),
        out_shape=jax.ShapeDtypeStruct((m, n), out_dtype),
        scratch_shapes=[pltpu.VMEM((k, tn), BF16)],
        compiler_params=_params("parallel", "arbitrary"),
        name="matmul",
    )(a, w)


def _matmul_bf16_kernel(a_ref, w_ref, o_ref):
    o_ref[...] = jnp.dot(a_ref[...], w_ref[...],
                         preferred_element_type=F32).astype(o_ref.dtype)


def _matmul_bf16(a, w, out_dtype, tm=1024, tn=1024):
    m, k = a.shape
    _, n = w.shape
    return pl.pallas_call(
        _matmul_bf16_kernel,
        grid=(n // tn, m // tm),
        in_specs=[pl.BlockSpec((tm, k), lambda j, i: (i, 0)),
                  pl.BlockSpec((k, tn), lambda j, i: (0, j))],
        out_specs=pl.BlockSpec((tm, tn), lambda j, i: (i, j)),
        out_shape=jax.ShapeDtypeStruct((m, n), out_dtype),
        compiler_params=_params("parallel", "parallel"),
        name="matmul_bf16",
    )(a, w)


def _matmul_residual_kernel(a_ref, w_ref, r_ref, o_ref):
    o_ref[...] = r_ref[...] + jnp.dot(a_ref[...], w_ref[...], preferred_element_type=F32)


def _matmul_residual(a, w, r, tm=1024, tn=1024):
    m, k = a.shape
    _, n = w.shape
    return pl.pallas_call(
        _matmul_residual_kernel,
        grid=(n // tn, m // tm),
        in_specs=[pl.BlockSpec((tm, k), lambda j, i: (i, 0)),
                  pl.BlockSpec((k, tn), lambda j, i: (0, j)),
                  pl.BlockSpec((tm, tn), lambda j, i: (i, j))],
        out_specs=pl.BlockSpec((tm, tn), lambda j, i: (i, j)),
        out_shape=jax.ShapeDtypeStruct((m, n), F32),
        compiler_params=_params("parallel", "parallel"),
        name="matmul_residual",
    )(a, w, r)


def _conv_proj_kernel(a_ref, wb_ref, wc_ref, wu_ref, cw_ref, o_ref,
                      wb_bf, wc_bf, wu_bf, hist_ref, *, tiles_per_seq):
    _cast_at_first_row_tile([(wb_ref, wb_bf), (wc_ref, wc_bf), (wu_ref, wu_bf)])

    @pl.when(pl.program_id(1) % tiles_per_seq == 0)
    def _():
        hist_ref[...] = jnp.zeros_like(hist_ref)

    a = a_ref[...]
    z = (jnp.dot(a, wc_bf[...], preferred_element_type=F32)
         * jnp.dot(a, wu_bf[...], preferred_element_type=F32))
    hist = hist_ref[...]
    n_hist = hist.shape[0]
    row = lax.broadcasted_iota(jnp.int32, hist.shape, 0)
    w = cw_ref[...]
    acc = z * w[2:3, :]
    for shift in (1, 2):
        zs = pltpu.roll(z, shift, axis=0)
        top = jnp.where(row >= shift, zs[:n_hist], pltpu.roll(hist, shift, axis=0))
        zs = jnp.concatenate([top, zs[n_hist:]], axis=0)
        acc = acc + zs * w[2 - shift:3 - shift, :]
    hist_ref[...] = z[z.shape[0] - n_hist:]
    o_ref[...] = (jnp.dot(a, wb_bf[...], preferred_element_type=F32) * acc).astype(o_ref.dtype)


def _conv_proj(a, w_in, conv_w, seq, tm=1024, tn=256):
    m, k = a.shape
    nb = CONV_DIM // tn
    wspec = lambda off: pl.BlockSpec((k, tn), lambda j, i: (0, off // tn + j))
    return pl.pallas_call(
        functools.partial(_conv_proj_kernel, tiles_per_seq=seq // tm),
        grid=(nb, m // tm),
        in_specs=[pl.BlockSpec((tm, k), lambda j, i: (i, 0)),
                  wspec(COL_B), wspec(COL_C), wspec(COL_U),
                  pl.BlockSpec((CONV_WIDTH, tn), lambda j, i: (0, j))],
        out_specs=pl.BlockSpec((tm, tn), lambda j, i: (i, j)),
        out_shape=jax.ShapeDtypeStruct((m, CONV_DIM), BF16),
        scratch_shapes=[pltpu.VMEM((k, tn), BF16)] * 3 + [pltpu.VMEM((SUBLANES, tn), F32)],
        compiler_params=_params("parallel", "arbitrary"),
        name="conv_proj",
    )(a, w_in, w_in, w_in, conv_w)


def _moba_kernel(q_ref, k_ref, v_ref, tab_ref, *refs, n_blk, n_cast):
    cast_refs, (o_ref, tab_t_ref), cast_bf_refs = refs[:n_cast], refs[n_cast:n_cast + 2], refs[n_cast + 2:]
    tab_t_ref[...] = tab_ref[...].T.astype(tab_t_ref.dtype)
    for src, dst in zip(cast_refs, cast_bf_refs):
        dst[...] = src[...].astype(dst.dtype)
    seq = q_ref.shape[0]
    nt_dims = (((1,), (1,)), ((), ()))
    k = k_ref[...]
    kb = k.astype(BF16)
    vb = v_ref[...].astype(BF16)
    q_all = q_ref[...]
    k_mean = jnp.mean(k.reshape(n_blk, MOBA_BLOCK, HEAD_DIM), axis=1)

    gate = lax.dot_general(k_mean, q_all, nt_dims, precision=lax.Precision.HIGHEST,
                           preferred_element_type=F32)
    blk = lax.broadcasted_iota(jnp.int32, (n_blk, seq), 0)
    own = lax.broadcasted_iota(jnp.int32, (n_blk, seq), 1) // MOBA_BLOCK
    past = blk < own
    gate = jnp.where(past, gate, NEG_INF)
    rank = jnp.zeros((n_blk, seq), jnp.int32)
    for jp in range(n_blk - 1):
        gj = gate[jp:jp + 1, :]
        ahead = (gj > gate) | ((gj == gate) & (blk > jp))
        rank = rank + ahead.astype(jnp.int32)
    sel = jnp.where((rank < MOBA_TOPK) & past, 1.0, 0.0).astype(BF16)
    expand = jnp.where(lax.broadcasted_iota(jnp.int32, (n_blk, seq), 1) // MOBA_BLOCK == blk,
                       1.0, 0.0).astype(BF16)

    row = lax.broadcasted_iota(jnp.int32, (MOBA_BLOCK, MOBA_BLOCK), 0)
    col = lax.broadcasted_iota(jnp.int32, (MOBA_BLOCK, MOBA_BLOCK), 1)
    causal = col <= row
    q_scale = HEAD_DIM ** -0.5 * 1.4426950408889634
    for i in range(n_blk):
        lo, nk = i * MOBA_BLOCK, (i + 1) * MOBA_BLOCK
        q = (q_all[lo:nk] * q_scale).astype(BF16)
        s = lax.dot_general(q, kb[:nk], nt_dims, preferred_element_type=F32)
        s_own = jnp.where(causal, s[:, lo:nk], NEG_INF)
        if i > 0:
            keep = lax.dot_general(sel[:, lo:nk], expand[:, :lo], (((0,), (0,)), ((), ())),
                                   preferred_element_type=F32)
            s = jnp.concatenate([jnp.where(keep > 0.5, s[:, :lo], NEG_INF), s_own], axis=1)
        else:
            s = s_own
        m = jnp.max(s, axis=-1, keepdims=True)
        p = jnp.exp2(s - m)
        l = jnp.sum(p, axis=-1, keepdims=True)
        out = jnp.dot(p.astype(BF16), vb[:nk], preferred_element_type=F32) / l
        o_ref[lo:nk, :] = out.astype(o_ref.dtype)


def _moba_attention(z, table, casts, batch, seq):
    n_blk = seq // MOBA_BLOCK
    steps = batch * N_HEADS
    step = lambda b, h: b * N_HEADS + h
    slab = lambda x: x.shape[0] // steps
    row_slab = lambda x: pl.BlockSpec((slab(x), x.shape[1]), lambda b, h: (step(b, h), 0))
    spec = lambda off: pl.BlockSpec((seq, HEAD_DIM), lambda b, h: (b, off // HEAD_DIM + h))
    rows, cols = table.shape
    return pl.pallas_call(
        functools.partial(_moba_kernel, n_blk=n_blk, n_cast=len(casts)),
        grid=(batch, N_HEADS),
        in_specs=[spec(Z_Q), spec(Z_K), spec(Z_V), row_slab(table)] + [row_slab(x) for x in casts],
        out_specs=[pl.BlockSpec((seq, HEAD_DIM), lambda b, h: (b, h)),
                   pl.BlockSpec((cols, slab(table)), lambda b, h: (0, step(b, h)))]
                  + [row_slab(x) for x in casts],
        out_shape=[jax.ShapeDtypeStruct((batch * seq, ATTN_DIM), BF16),
                   jax.ShapeDtypeStruct((cols, rows), BF16)]
                  + [jax.ShapeDtypeStruct(x.shape, BF16) for x in casts],
        compiler_params=_params("parallel", "parallel"),
        name="moba_attention",
    )(z, z, z, table, *casts)


def _merge_kernel(bz_ref, at_ref, wc_ref, wa_ref, gc_ref, ga_ref, bc_ref, ba_ref, o_ref):
    y_conv = jnp.dot(bz_ref[...], wc_ref[...], preferred_element_type=F32)
    y_attn = jnp.dot(at_ref[...], wa_ref[...], preferred_element_type=F32)
    merged = (jax.nn.sigmoid(gc_ref[...] + bc_ref[...]) * y_conv
              + jax.nn.sigmoid(ga_ref[...] + ba_ref[...]) * y_attn)
    o_ref[...] = merged.astype(o_ref.dtype)


def _merge(bz, attn, w_conv_out, w_attn_out, z, b_gate, tm=512, tn=1024):
    m = bz.shape[0]
    nb = D_MODEL // tn
    bias = b_gate.reshape(1, 2 * D_MODEL)
    return pl.pallas_call(
        _merge_kernel,
        grid=(nb, m // tm),
        in_specs=[pl.BlockSpec((tm, CONV_DIM), lambda j, i: (i, 0)),
                  pl.BlockSpec((tm, ATTN_DIM), lambda j, i: (i, 0)),
                  pl.BlockSpec((CONV_DIM, tn), lambda j, i: (0, j)),
                  pl.BlockSpec((ATTN_DIM, tn), lambda j, i: (0, j)),
                  pl.BlockSpec((tm, tn), lambda j, i: (i, Z_GC // tn + j)),
                  pl.BlockSpec((tm, tn), lambda j, i: (i, Z_GA // tn + j)),
                  pl.BlockSpec((1, tn), lambda j, i: (0, j)),
                  pl.BlockSpec((1, tn), lambda j, i: (0, nb + j))],
        out_specs=pl.BlockSpec((tm, tn), lambda j, i: (i, j)),
        out_shape=jax.ShapeDtypeStruct((m, D_MODEL), BF16),
        compiler_params=_params("parallel", "parallel"),
        name="merge",
    )(bz, attn, w_conv_out, w_attn_out, z, z, bias, bias)


def _sort16_pairs():
    pairs = []

    def merge(lo, n, r):
        step = r * 2
        if step < n:
            merge(lo, n, step)
            merge(lo + r, n, step)
            pairs.extend((i, i + r) for i in range(lo + r, lo + n - r, step))
        else:
            pairs.append((lo, lo + r))

    def sort(lo, n):
        if n > 1:
            sort(lo, n // 2)
            sort(lo + n // 2, n // 2)
            merge(lo, n, 1)

    sort(0, PEER_TOPK)
    return pairs


_SORT16 = _sort16_pairs()


def _compare_exchange(vals, i, j):
    a, b = vals[i], vals[j]
    if b is None:
        return
    if a is None:
        vals[i], vals[j] = b, None
        return
    vals[i], vals[j] = jnp.maximum(a, b), jnp.minimum(a, b)


def _top16(blocks):
    vals = list(blocks) + [None] * (PEER_TOPK - len(blocks))
    for i, j in _SORT16:
        _compare_exchange(vals, i, j)
    for shift in (4, 2, 1):
        other = [None if v is None else pltpu.roll(v, shift, axis=0) for v in vals]
        merged = []
        for i in range(PEER_TOPK):
            a, b = vals[i], other[PEER_TOPK - 1 - i]
            merged.append(b if a is None else a if b is None else jnp.maximum(a, b))
        vals = merged
        for step in (8, 4, 2, 1):
            for i in range(PEER_TOPK):
                if not i & step:
                    _compare_exchange(vals, i, i + step)
    return vals


def _sublane_sum(x):
    for shift in (4, 2, 1):
        x = x + pltpu.roll(x, shift, axis=0)
    return x


def _route_kernel(q_ref, sk_ref, n1_ref, rank2_ref, e1_ref, e2_ref):
    q = q_ref[...]
    tm = q.shape[0]
    dims = (((1,), (1,)), ((), ()))
    s1 = lax.dot_general(sk_ref[0, 0], q[:, :PEER_HALF], dims,
                         precision=lax.Precision.HIGHEST, preferred_element_type=F32)
    s2 = lax.dot_general(sk_ref[0, 1], q[:, PEER_HALF:], dims,
                         precision=lax.Precision.HIGHEST, preferred_element_type=F32)
    n_blocks = N_KEYS // SUBLANES
    s1_blocks = [s1[k * SUBLANES:(k + 1) * SUBLANES, :] for k in range(n_blocks)]
    s2_blocks = [s2[k * SUBLANES:(k + 1) * SUBLANES, :] for k in range(n_blocks)]
    t1 = _top16(s1_blocks)
    t2 = _top16(s2_blocks)

    sub = lax.broadcasted_iota(jnp.int32, (SUBLANES, tm), 0)
    spread = lambda rows: functools.reduce(
        lambda acc, b: jnp.where(sub == b, rows[b], acc), range(SUBLANES - 1), rows[SUBLANES - 1])
    t2_lo, t2_hi, t1_hi = spread(t2[:SUBLANES]), spread(t2[SUBLANES:]), spread(t1[SUBLANES:])
    cand = [t1[0] + t2_lo, t1[0] + t2_hi]
    for a in range(1, SUBLANES):
        n_valid = PEER_TOPK // (a + 1)
        piece = t1[a] + t2_lo
        cand.append(piece if n_valid >= SUBLANES else jnp.where(sub < n_valid, piece, NEG_INF))
    cand.append(t1_hi + t2[0])
    best = _top16(cand)
    tau = best[PEER_TOPK - 1]
    z = jnp.ones_like(tau)
    for r in range(1, PEER_TOPK):
        z = z + jnp.exp(best[r] - best[0])
    inv_z = 1.0 / z

    count = [_sublane_sum(jnp.where(t1[a] + t2_lo >= tau, 1.0, 0.0)
                          + jnp.where(t1[a] + t2_hi >= tau, 1.0, 0.0)) for a in range(PEER_TOPK)]
    n1, rank2, e1, e2 = [], [], [], []
    for k in range(n_blocks):
        n1_k = jnp.zeros((SUBLANES, tm), F32)
        rank2_k = jnp.full((SUBLANES, tm), float(PEER_TOPK), F32)
        for a in reversed(range(PEER_TOPK)):
            n1_k = jnp.where(s1_blocks[k] == t1[a], count[a], n1_k)
            rank2_k = jnp.where(s2_blocks[k] == t2[a], float(a), rank2_k)
        n1.append(n1_k)
        rank2.append(rank2_k)
        e1.append(jnp.exp(s1_blocks[k] - t1[0]) * inv_z)
        e2.append(jnp.exp(s2_blocks[k] - t2[0]))
    n1_ref[0] = jnp.concatenate(n1, axis=0)
    rank2_ref[0] = jnp.concatenate(rank2, axis=0).astype(rank2_ref.dtype)
    e1_ref[0] = jnp.concatenate(e1, axis=0)
    e2_ref[0] = jnp.concatenate(e2, axis=0).astype(e2_ref.dtype)


def _peer_route(qp, sub_keys, tm=512):
    t = qp.shape[0]
    key_out = pl.BlockSpec((1, N_KEYS, tm), lambda i, h: (h, 0, i))
    key_shape = lambda dtype: jax.ShapeDtypeStruct((PEER_HEADS, N_KEYS, t), dtype)
    return pl.pallas_call(
        _route_kernel,
        grid=(t // tm, PEER_HEADS),
        in_specs=[pl.BlockSpec((tm, 2 * PEER_HALF), lambda i, h: (i, h)),
                  pl.BlockSpec((1, 2, N_KEYS, PEER_HALF), lambda i, h: (h, 0, 0, 0))],
        out_specs=[key_out, key_out, key_out, key_out],
        out_shape=[key_shape(F32), key_shape(BF16), key_shape(F32), key_shape(BF16)],
        compiler_params=_params("parallel", "parallel"),
        name="peer_route",
    )(qp, sub_keys)


def _peer_kernel(xt_ref, u_ref, vt_ref, n1_ref, rank2_ref, e1_ref, e2_ref, o_ref, *, n_grp):
    c = pl.program_id(1)

    @pl.when(c == 0)
    def _():
        o_ref[...] = jnp.zeros_like(o_ref)

    hid = jnp.dot(u_ref[...], xt_ref[...], preferred_element_type=F32)
    sqrt_half = 0.7071067811865476
    tt = hid.shape[1]
    pack = 2 * SUBLANES
    w_parts = []
    for g in range(n_grp):
        i = c * n_grp + g
        spread = lambda ref, h: jnp.broadcast_to(ref[h, pl.ds(i, 1), :], (pack, tt)).astype(BF16)
        n1 = [spread(n1_ref, h) for h in range(PEER_HEADS)]
        e1 = [spread(e1_ref, h) for h in range(PEER_HEADS)]
        for lo in range(0, N_KEYS, pack):
            keys = slice(lo, lo + pack)
            gate = None
            for h in range(PEER_HEADS):
                contrib = jnp.where(rank2_ref[h, keys, :] < n1[h], e1[h] * e2_ref[h, keys, :],
                                    jnp.zeros((), BF16))
                gate = contrib if gate is None else gate + contrib
            hg = hid[g * N_KEYS + lo:g * N_KEYS + lo + pack, :]
            act = 0.5 * hg * (1.0 + lax.erf(hg * sqrt_half))
            w_parts.append(act.astype(BF16) * gate)
    w = jnp.concatenate(w_parts, axis=0)
    o_ref[...] += jnp.dot(vt_ref[...], w, preferred_element_type=F32)


def _peer_mix(xnt, u, vt, n1, rank2, e1, e2, tt=512, n_grp=4):
    d, t = xnt.shape
    ec = n_grp * N_KEYS
    once = pl.Buffered(1)
    tok = pl.BlockSpec((PEER_HEADS, N_KEYS, tt), lambda i, c: (0, 0, i), pipeline_mode=once)
    return pl.pallas_call(
        functools.partial(_peer_kernel, n_grp=n_grp),
        grid=(t // tt, N_EXPERTS // ec),
        in_specs=[pl.BlockSpec((d, tt), lambda i, c: (0, i), pipeline_mode=once),
                  pl.BlockSpec((ec, d), lambda i, c: (c, 0)),
                  pl.BlockSpec((d, ec), lambda i, c: (0, c)),
                  tok, tok, tok, tok],
        out_specs=pl.BlockSpec((d, tt), lambda i, c: (0, i)),
        out_shape=jax.ShapeDtypeStruct((d, t), F32),
        compiler_params=_params("parallel", "arbitrary"),
        name="peer_mix",
    )(xnt, u, vt, n1, rank2, e1, e2)


def kernel(x, norm_mix, w_in, b_gate, conv_w, w_conv_out, w_attn_out, w_o, norm_ffn,
           w_peer_q, sub_keys, u_emb, v_emb, norm_final):
    batch, seq, d = x.shape
    assert w_in.shape[0] == 1, "single-layer block"
    h = x.reshape(batch * seq, d)
    hn = _rmsnorm(h, norm_mix[0], BF16)
    bz = _conv_proj(hn, w_in[0], conv_w[0], seq)
    z = _matmul(hn, w_in[0], F32, col0=COL_Q, n=IN_COLS - COL_Q)
    attn, v_t, u_bf, w_o_bf, w_co_bf, w_ao_bf, w_q_bf = _moba_attention(
        z, v_emb[0], [u_emb[0], w_o[0], w_conv_out[0], w_attn_out[0], w_peer_q[0]], batch, seq)
    merged = _merge(bz, attn, w_co_bf, w_ao_bf, z, b_gate[0])
    h = _matmul_residual(merged, w_o_bf, h)
    hn, hnt = _rmsnorm_with_transpose(h, norm_ffn[0])
    qp = _matmul_bf16(hn, w_q_bf, F32)
    n1, rank2, e1, e2 = _peer_route(qp, sub_keys[0])
    peer_t = _peer_mix(hnt, u_bf, v_t, n1, rank2, e1, e2)
    return _add_rmsnorm(h, peer_t, norm_final).reshape(batch, seq, d)
```

```python
import functools

import jax
import jax.numpy as jnp
from jax import lax
from jax.experimental import pallas as pl
from jax.experimental.pallas import tpu as pltpu

D_MODEL = 4096
CONV_DIM = 2048
CONV_WIDTH = 3
N_HEADS = 16
HEAD_DIM = 128
ATTN_DIM = N_HEADS * HEAD_DIM
MOBA_BLOCK = 256
MOBA_TOPK = 3
PEER_HEADS = 8
N_KEYS = 128
N_EXPERTS = N_KEYS * N_KEYS
PEER_HALF = 128
PEER_TOPK = 16
EPS = 1e-6

COL_B = 0
COL_C = CONV_DIM
COL_U = 2 * CONV_DIM
COL_Q = 3 * CONV_DIM
COL_K = COL_Q + ATTN_DIM
COL_V = COL_K + ATTN_DIM
COL_GC = COL_V + ATTN_DIM
COL_GA = COL_GC + D_MODEL
IN_COLS = COL_GA + D_MODEL
Z_Q = 0
Z_K = COL_K - COL_Q
Z_V = COL_V - COL_Q
Z_GC = COL_GC - COL_Q
Z_GA = COL_GA - COL_Q

VMEM_LIMIT_BYTES = 56 * 1024 * 1024

F32 = jnp.float32
BF16 = jnp.bfloat16
NEG_INF = float("-inf")
SUBLANES = 8


def _params(*semantics):
    return pltpu.CompilerParams(dimension_semantics=semantics,
                                vmem_limit_bytes=VMEM_LIMIT_BYTES)


def _rmsnorm_kernel(x_ref, g_ref, o_ref):
    x = x_ref[...]
    y = x * lax.rsqrt(jnp.mean(x * x, axis=-1, keepdims=True) + EPS)
    o_ref[...] = (y * g_ref[...]).astype(o_ref.dtype)


def _rmsnorm(x, g, out_dtype, tm=256):
    m, d = x.shape
    return pl.pallas_call(
        _rmsnorm_kernel,
        grid=(m // tm,),
        in_specs=[pl.BlockSpec((tm, d), lambda i: (i, 0)),
                  pl.BlockSpec((1, d), lambda i: (0, 0))],
        out_specs=pl.BlockSpec((tm, d), lambda i: (i, 0)),
        out_shape=jax.ShapeDtypeStruct((m, d), out_dtype),
        compiler_params=_params("parallel"),
        name="rmsnorm",
    )(x, g.reshape(1, d))


def _rmsnorm_t_kernel(x_ref, g_ref, o_ref, ot_ref):
    x = x_ref[...]
    y = x * lax.rsqrt(jnp.mean(x * x, axis=-1, keepdims=True) + EPS) * g_ref[...]
    o_ref[...] = y.astype(o_ref.dtype)
    ot_ref[...] = y.T.astype(ot_ref.dtype)


def _rmsnorm_with_transpose(x, g, tm=256):
    m, d = x.shape
    return pl.pallas_call(
        _rmsnorm_t_kernel,
        grid=(m // tm,),
        in_specs=[pl.BlockSpec((tm, d), lambda i: (i, 0)),
                  pl.BlockSpec((1, d), lambda i: (0, 0))],
        out_specs=[pl.BlockSpec((tm, d), lambda i: (i, 0)),
                   pl.BlockSpec((d, tm), lambda i: (0, i))],
        out_shape=[jax.ShapeDtypeStruct((m, d), BF16),
                   jax.ShapeDtypeStruct((d, m), BF16)],
        compiler_params=_params("parallel"),
        name="rmsnorm_t",
    )(x, g.reshape(1, d))


def _add_rmsnorm_kernel(a_ref, bt_ref, g_ref, o_ref):
    x = a_ref[...] + bt_ref[...].T
    y = x * lax.rsqrt(jnp.mean(x * x, axis=-1, keepdims=True) + EPS)
    o_ref[...] = (y * g_ref[...]).astype(o_ref.dtype)


def _add_rmsnorm(a, bt, g, tm=256):
    m, d = a.shape
    return pl.pallas_call(
        _add_rmsnorm_kernel,
        grid=(m // tm,),
        in_specs=[pl.BlockSpec((tm, d), lambda i: (i, 0)),
                  pl.BlockSpec((d, tm), lambda i: (0, i)),
                  pl.BlockSpec((1, d), lambda i: (0, 0))],
        out_specs=pl.BlockSpec((tm, d), lambda i: (i, 0)),
        out_shape=jax.ShapeDtypeStruct((m, d), F32),
        compiler_params=_params("parallel"),
        name="add_rmsnorm",
    )(a, bt, g.reshape(1, d))


def _cast_at_first_row_tile(pairs):
    @pl.when(pl.program_id(1) == 0)
    def _():
        for src, dst in pairs:
            dst[...] = src[...].astype(dst.dtype)


def _matmul_kernel(a_ref, w_ref, o_ref, wb_ref):
    _cast_at_first_row_tile([(w_ref, wb_ref)])
    o_ref[...] = jnp.dot(a_ref[...], wb_ref[...],
                         preferred_element_type=F32).astype(o_ref.dtype)


def _matmul(a, w, out_dtype, col0, n, tm=512, tn=1024):
    m, k = a.shape
    return pl.pallas_call(
        _matmul_kernel,
        grid=(n // tn, m // tm),
        in_specs=[pl.BlockSpec((tm, k), lambda j, i: (i, 0)),
                  pl.BlockSpec((k, tn), lambda j, i: (0, col0 // tn + j))],
        out_specs=pl.BlockSpec((tm, tn), lambda j, i: (i, j)),
        out_shape=jax.ShapeDtypeStruct((m, n), out_dtype),
        scratch_shapes=[pltpu.VMEM((k, tn), BF16)],
        compiler_params=_params("parallel", "arbitrary"),
        name="matmul",
    )(a, w)


def _matmul_bf16_kernel(a_ref, w_ref, o_ref):
    o_ref[...] = jnp.dot(a_ref[...], w_ref[...],
                         preferred_element_type=F32).astype(o_ref.dtype)


def _matmul_bf16(a, w, out_dtype, tm=1024, tn=1024):
    m, k = a.shape
    _, n = w.shape
    return pl.pallas_call(
        _matmul_bf16_kernel,
        grid=(n // tn, m // tm),
        in_specs=[pl.BlockSpec((tm, k), lambda j, i: (i, 0)),
                  pl.BlockSpec((k, tn), lambda j, i: (0, j))],
        out_specs=pl.BlockSpec((tm, tn), lambda j, i: (i, j)),
        out_shape=jax.ShapeDtypeStruct((m, n), out_dtype),
        compiler_params=_params("parallel", "parallel"),
        name="matmul_bf16",
    )(a, w)


def _matmul_residual_kernel(a_ref, w_ref, r_ref, o_ref):
    o_ref[...] = r_ref[...] + jnp.dot(a_ref[...], w_ref[...], preferred_element_type=F32)


def _matmul_residual(a, w, r, tm=1024, tn=1024):
    m, k = a.shape
    _, n = w.shape
    return pl.pallas_call(
        _matmul_residual_kernel,
        grid=(n // tn, m // tm),
        in_specs=[pl.BlockSpec((tm, k), lambda j, i: (i, 0)),
                  pl.BlockSpec((k, tn), lambda j, i: (0, j)),
                  pl.BlockSpec((tm, tn), lambda j, i: (i, j))],
        out_specs=pl.BlockSpec((tm, tn), lambda j, i: (i, j)),
        out_shape=jax.ShapeDtypeStruct((m, n), F32),
        compiler_params=_params("parallel", "parallel"),
        name="matmul_residual",
    )(a, w, r)


def _conv_proj_kernel(a_ref, wb_ref, wc_ref, wu_ref, cw_ref, o_ref,
                      wb_bf, wc_bf, wu_bf, hist_ref, *, tiles_per_seq):
    _cast_at_first_row_tile([(wb_ref, wb_bf), (wc_ref, wc_bf), (wu_ref, wu_bf)])

    @pl.when(pl.program_id(1) % tiles_per_seq == 0)
    def _():
        hist_ref[...] = jnp.zeros_like(hist_ref)

    a = a_ref[...]
    z = (jnp.dot(a, wc_bf[...], preferred_element_type=F32)
         * jnp.dot(a, wu_bf[...], preferred_element_type=F32))
    hist = hist_ref[...]
    n_hist = hist.shape[0]
    row = lax.broadcasted_iota(jnp.int32, hist.shape, 0)
    w = cw_ref[...]
    acc = z * w[2:3, :]
    for shift in (1, 2):
        zs = pltpu.roll(z, shift, axis=0)
        top = jnp.where(row >= shift, zs[:n_hist], pltpu.roll(hist, shift, axis=0))
        zs = jnp.concatenate([top, zs[n_hist:]], axis=0)
        acc = acc + zs * w[2 - shift:3 - shift, :]
    hist_ref[...] = z[z.shape[0] - n_hist:]
    o_ref[...] = (jnp.dot(a, wb_bf[...], preferred_element_type=F32) * acc).astype(o_ref.dtype)


def _conv_proj(a, w_in, conv_w, seq, tm=1024, tn=256):
    m, k = a.shape
    nb = CONV_DIM // tn
    wspec = lambda off: pl.BlockSpec((k, tn), lambda j, i: (0, off // tn + j))
    return pl.pallas_call(
        functools.partial(_conv_proj_kernel, tiles_per_seq=seq // tm),
        grid=(nb, m // tm),
        in_specs=[pl.BlockSpec((tm, k), lambda j, i: (i, 0)),
                  wspec(COL_B), wspec(COL_C), wspec(COL_U),
                  pl.BlockSpec((CONV_WIDTH, tn), lambda j, i: (0, j))],
        out_specs=pl.BlockSpec((tm, tn), lambda j, i: (i, j)),
        out_shape=jax.ShapeDtypeStruct((m, CONV_DIM), BF16),
        scratch_shapes=[pltpu.VMEM((k, tn), BF16)] * 3 + [pltpu.VMEM((SUBLANES, tn), F32)],
        compiler_params=_params("parallel", "arbitrary"),
        name="conv_proj",
    )(a, w_in, w_in, w_in, conv_w)


def _moba_kernel(q_ref, k_ref, v_ref, tab_ref, *refs, n_blk, n_cast):
    cast_refs, (o_ref, tab_t_ref), cast_bf_refs = refs[:n_cast], refs[n_cast:n_cast + 2], refs[n_cast + 2:]
    tab_t_ref[...] = tab_ref[...].T.astype(tab_t_ref.dtype)
    for src, dst in zip(cast_refs, cast_bf_refs):
        dst[...] = src[...].astype(dst.dtype)
    seq = q_ref.shape[0]
    nt_dims = (((1,), (1,)), ((), ()))
    k = k_ref[...]
    kb = k.astype(BF16)
    vb = v_ref[...].astype(BF16)
    q_all = q_ref[...]
    k_mean = jnp.mean(k.reshape(n_blk, MOBA_BLOCK, HEAD_DIM), axis=1)

    gate = lax.dot_general(k_mean, q_all, nt_dims, precision=lax.Precision.HIGHEST,
                           preferred_element_type=F32)
    blk = lax.broadcasted_iota(jnp.int32, (n_blk, seq), 0)
    own = lax.broadcasted_iota(jnp.int32, (n_blk, seq), 1) // MOBA_BLOCK
    past = blk < own
    gate = jnp.where(past, gate, NEG_INF)
    rank = jnp.zeros((n_blk, seq), jnp.int32)
    for jp in range(n_blk - 1):
        gj = gate[jp:jp + 1, :]
        ahead = (gj > gate) | ((gj == gate) & (blk > jp))
        rank = rank + ahead.astype(jnp.int32)
    sel = jnp.where((rank < MOBA_TOPK) & past, 1.0, 0.0).astype(BF16)
    expand = jnp.where(lax.broadcasted_iota(jnp.int32, (n_blk, seq), 1) // MOBA_BLOCK == blk,
                       1.0, 0.0).astype(BF16)

    row = lax.broadcasted_iota(jnp.int32, (MOBA_BLOCK, MOBA_BLOCK), 0)
    col = lax.broadcasted_iota(jnp.int32, (MOBA_BLOCK, MOBA_BLOCK), 1)
    causal = col <= row
    q_scale = HEAD_DIM ** -0.5 * 1.4426950408889634
    for i in range(n_blk):
        lo, nk = i * MOBA_BLOCK, (i + 1) * MOBA_BLOCK
        q = (q_all[lo:nk] * q_scale).astype(BF16)
        s = lax.dot_general(q, kb[:nk], nt_dims, preferred_element_type=F32)
        s_own = jnp.where(causal, s[:, lo:nk], NEG_INF)
        if i > 0:
            keep = lax.dot_general(sel[:, lo:nk], expand[:, :lo], (((0,), (0,)), ((), ())),
                                   preferred_element_type=F32)
            s = jnp.concatenate([jnp.where(keep > 0.5, s[:, :lo], NEG_INF), s_own], axis=1)
        else:
            s = s_own
        m = jnp.max(s, axis=-1, keepdims=True)
        p = jnp.exp2(s - m)
        l = jnp.sum(p, axis=-1, keepdims=True)
        out = jnp.dot(p.astype(BF16), vb[:nk], preferred_element_type=F32) / l
        o_ref[lo:nk, :] = out.astype(o_ref.dtype)


def _moba_attention(z, table, casts, batch, seq):
    n_blk = seq // MOBA_BLOCK
    steps = batch * N_HEADS
    step = lambda b, h: b * N_HEADS + h
    slab = lambda x: x.shape[0] // steps
    row_slab = lambda x: pl.BlockSpec((slab(x), x.shape[1]), lambda b, h: (step(b, h), 0))
    spec = lambda off: pl.BlockSpec((seq, HEAD_DIM), lambda b, h: (b, off // HEAD_DIM + h))
    rows, cols = table.shape
    return pl.pallas_call(
        functools.partial(_moba_kernel, n_blk=n_blk, n_cast=len(casts)),
        grid=(batch, N_HEADS),
        in_specs=[spec(Z_Q), spec(Z_K), spec(Z_V), row_slab(table)] + [row_slab(x) for x in casts],
        out_specs=[pl.BlockSpec((seq, HEAD_DIM), lambda b, h: (b, h)),
                   pl.BlockSpec((cols, slab(table)), lambda b, h: (0, step(b, h)))]
                  + [row_slab(x) for x in casts],
        out_shape=[jax.ShapeDtypeStruct((batch * seq, ATTN_DIM), BF16),
                   jax.ShapeDtypeStruct((cols, rows), BF16)]
                  + [jax.ShapeDtypeStruct(x.shape, BF16) for x in casts],
        compiler_params=_params("parallel", "parallel"),
        name="moba_attention",
    )(z, z, z, table, *casts)


def _merge_kernel(bz_ref, at_ref, wc_ref, wa_ref, gc_ref, ga_ref, bc_ref, ba_ref, o_ref):
    y_conv = jnp.dot(bz_ref[...], wc_ref[...], preferred_element_type=F32)
    y_attn = jnp.dot(at_ref[...], wa_ref[...], preferred_element_type=F32)
    merged = (jax.nn.sigmoid(gc_ref[...] + bc_ref[...]) * y_conv
              + jax.nn.sigmoid(ga_ref[...] + ba_ref[...]) * y_attn)
    o_ref[...] = merged.astype(o_ref.dtype)


def _merge(bz, attn, w_conv_out, w_attn_out, z, b_gate, tm=512, tn=1024):
    m = bz.shape[0]
    nb = D_MODEL // tn
    bias = b_gate.reshape(1, 2 * D_MODEL)
    return pl.pallas_call(
        _merge_kernel,
        grid=(nb, m // tm),
        in_specs=[pl.BlockSpec((tm, CONV_DIM), lambda j, i: (i, 0)),
                  pl.BlockSpec((tm, ATTN_DIM), lambda j, i: (i, 0)),
                  pl.BlockSpec((CONV_DIM, tn), lambda j, i: (0, j)),
                  pl.BlockSpec((ATTN_DIM, tn), lambda j, i: (0, j)),
                  pl.BlockSpec((tm, tn), lambda j, i: (i, Z_GC // tn + j)),
                  pl.BlockSpec((tm, tn), lambda j, i: (i, Z_GA // tn + j)),
                  pl.BlockSpec((1, tn), lambda j, i: (0, j)),
                  pl.BlockSpec((1, tn), lambda j, i: (0, nb + j))],
        out_specs=pl.BlockSpec((tm, tn), lambda j, i: (i, j)),
        out_shape=jax.ShapeDtypeStruct((m, D_MODEL), BF16),
        compiler_params=_params("parallel", "parallel"),
        name="merge",
    )(bz, attn, w_conv_out, w_attn_out, z, z, bias, bias)


def _sort16_pairs():
    pairs = []

    def merge(lo, n, r):
        step = r * 2
        if step < n:
            merge(lo, n, step)
            merge(lo + r, n, step)
            pairs.extend((i, i + r) for i in range(lo + r, lo + n - r, step))
        else:
            pairs.append((lo, lo + r))

    def sort(lo, n):
        if n > 1:
            sort(lo, n // 2)
            sort(lo + n // 2, n // 2)
            merge(lo, n, 1)

    sort(0, PEER_TOPK)
    return pairs


_SORT16 = _sort16_pairs()


def _compare_exchange(vals, i, j):
    a, b = vals[i], vals[j]
    if b is None:
        return
    if a is None:
        vals[i], vals[j] = b, None
        return
    vals[i], vals[j] = jnp.maximum(a, b), jnp.minimum(a, b)


def _top16(blocks):
    vals = list(blocks) + [None] * (PEER_TOPK - len(blocks))
    for i, j in _SORT16:
        _compare_exchange(vals, i, j)
    for shift in (4, 2, 1):
        other = [None if v is None else pltpu.roll(v, shift, axis=0) for v in vals]
        merged = []
        for i in range(PEER_TOPK):
            a, b = vals[i], other[PEER_TOPK - 1 - i]
            merged.append(b if a is None else a if b is None else jnp.maximum(a, b))
        vals = merged
        for step in (8, 4, 2, 1):
            for i in range(PEER_TOPK):
                if not i & step:
                    _compare_exchange(vals, i, i + step)
    return vals


def _sublane_sum(x):
    for shift in (4, 2, 1):
        x = x + pltpu.roll(x, shift, axis=0)
    return x


def _route_kernel(q_ref, sk_ref, n1_ref, rank2_ref, e1_ref, e2_ref):
    q = q_ref[...]
    tm = q.shape[0]
    dims = (((1,), (1,)), ((), ()))
    s1 = lax.dot_general(sk_ref[0, 0], q[:, :PEER_HALF], dims,
                         precision=lax.Precision.HIGHEST, preferred_element_type=F32)
    s2 = lax.dot_general(sk_ref[0, 1], q[:, PEER_HALF:], dims,
                         precision=lax.Precision.HIGHEST, preferred_element_type=F32)
    n_blocks = N_KEYS // SUBLANES
    s1_blocks = [s1[k * SUBLANES:(k + 1) * SUBLANES, :] for k in range(n_blocks)]
    s2_blocks = [s2[k * SUBLANES:(k + 1) * SUBLANES, :] for k in range(n_blocks)]
    t1 = _top16(s1_blocks)
    t2 = _top16(s2_blocks)

    sub = lax.broadcasted_iota(jnp.int32, (SUBLANES, tm), 0)
    spread = lambda rows: functools.reduce(
        lambda acc, b: jnp.where(sub == b, rows[b], acc), range(SUBLANES - 1), rows[SUBLANES - 1])
    t2_lo, t2_hi, t1_hi = spread(t2[:SUBLANES]), spread(t2[SUBLANES:]), spread(t1[SUBLANES:])
    cand = [t1[0] + t2_lo, t1[0] + t2_hi]
    for a in range(1, SUBLANES):
        n_valid = PEER_TOPK // (a + 1)
        piece = t1[a] + t2_lo
        cand.append(piece if n_valid >= SUBLANES else jnp.where(sub < n_valid, piece, NEG_INF))
    cand.append(t1_hi + t2[0])
    best = _top16(cand)
    tau = best[PEER_TOPK - 1]
    z = jnp.ones_like(tau)
    for r in range(1, PEER_TOPK):
        z = z + jnp.exp(best[r] - best[0])
    inv_z = 1.0 / z

    count = [_sublane_sum(jnp.where(t1[a] + t2_lo >= tau, 1.0, 0.0)
                          + jnp.where(t1[a] + t2_hi >= tau, 1.0, 0.0)) for a in range(PEER_TOPK)]
    n1, rank2, e1, e2 = [], [], [], []
    for k in range(n_blocks):
        n1_k = jnp.zeros((SUBLANES, tm), F32)
        rank2_k = jnp.full((SUBLANES, tm), float(PEER_TOPK), F32)
        for a in reversed(range(PEER_TOPK)):
            n1_k = jnp.where(s1_blocks[k] == t1[a], count[a], n1_k)
            rank2_k = jnp.where(s2_blocks[k] == t2[a], float(a), rank2_k)
        n1.append(n1_k)
        rank2.append(rank2_k)
        e1.append(jnp.exp(s1_blocks[k] - t1[0]) * inv_z)
        e2.append(jnp.exp(s2_blocks[k] - t2[0]))
    n1_ref[0] = jnp.concatenate(n1, axis=0)
    rank2_ref[0] = jnp.concatenate(rank2, axis=0).astype(rank2_ref.dtype)
    e1_ref[0] = jnp.concatenate(e1, axis=0)
    e2_ref[0] = jnp.concatenate(e2, axis=0).astype(e2_ref.dtype)


def _peer_route(qp, sub_keys, tm=1024):
    t = qp.shape[0]
    key_out = pl.BlockSpec((1, N_KEYS, tm), lambda i, h: (h, 0, i))
    key_shape = lambda dtype: jax.ShapeDtypeStruct((PEER_HEADS, N_KEYS, t), dtype)
    return pl.pallas_call(
        _route_kernel,
        grid=(t // tm, PEER_HEADS),
        in_specs=[pl.BlockSpec((tm, 2 * PEER_HALF), lambda i, h: (i, h)),
                  pl.BlockSpec((1, 2, N_KEYS, PEER_HALF), lambda i, h: (h, 0, 0, 0))],
        out_specs=[key_out, key_out, key_out, key_out],
        out_shape=[key_shape(F32), key_shape(BF16), key_shape(F32), key_shape(BF16)],
        compiler_params=_params("parallel", "parallel"),
        name="peer_route",
    )(qp, sub_keys)


def _peer_kernel(xt_ref, u_ref, vt_ref, n1_ref, rank2_ref, e1_ref, e2_ref, o_ref, *, n_grp):
    c = pl.program_id(1)

    @pl.when(c == 0)
    def _():
        o_ref[...] = jnp.zeros_like(o_ref)

    hid = jnp.dot(u_ref[...], xt_ref[...], preferred_element_type=F32)
    sqrt_half = 0.7071067811865476
    tt = hid.shape[1]
    pack = 2 * SUBLANES
    w_parts = []
    for g in range(n_grp):
        i = c * n_grp + g
        spread = lambda ref, h: jnp.broadcast_to(ref[h, pl.ds(i, 1), :], (pack, tt)).astype(BF16)
        n1 = [spread(n1_ref, h) for h in range(PEER_HEADS)]
        e1 = [spread(e1_ref, h) for h in range(PEER_HEADS)]
        for lo in range(0, N_KEYS, pack):
            keys = slice(lo, lo + pack)
            gate = None
            for h in range(PEER_HEADS):
                contrib = jnp.where(rank2_ref[h, keys, :] < n1[h], e1[h] * e2_ref[h, keys, :],
                                    jnp.zeros((), BF16))
                gate = contrib if gate is None else gate + contrib
            hg = hid[g * N_KEYS + lo:g * N_KEYS + lo + pack, :]
            act = 0.5 * hg * (1.0 + lax.erf(hg * sqrt_half))
            w_parts.append(act.astype(BF16) * gate)
    w = jnp.concatenate(w_parts, axis=0)
    o_ref[...] += jnp.dot(vt_ref[...], w, preferred_element_type=F32)


def _peer_mix(xnt, u, vt, n1, rank2, e1, e2, tt=512, n_grp=4):
    d, t = xnt.shape
    ec = n_grp * N_KEYS
    once = pl.Buffered(1)
    tok = pl.BlockSpec((PEER_HEADS, N_KEYS, tt), lambda i, c: (0, 0, i), pipeline_mode=once)
    return pl.pallas_call(
        functools.partial(_peer_kernel, n_grp=n_grp),
        grid=(t // tt, N_EXPERTS // ec),
        in_specs=[pl.BlockSpec((d, tt), lambda i, c: (0, i), pipeline_mode=once),
                  pl.BlockSpec((ec, d), lambda i, c: (c, 0)),
                  pl.BlockSpec((d, ec), lambda i, c: (0, c)),
                  tok, tok, tok, tok],
        out_specs=pl.BlockSpec((d, tt), lambda i, c: (0, i)),
        out_shape=jax.ShapeDtypeStruct((d, t), F32),
        compiler_params=_params("parallel", "arbitrary"),
        name="peer_mix",
    )(xnt, u, vt, n1, rank2, e1, e2)


def kernel(x, norm_mix, w_in, b_gate, conv_w, w_conv_out, w_attn_out, w_o, norm_ffn,
           w_peer_q, sub_keys, u_emb, v_emb, norm_final):
    batch, seq, d = x.shape
    assert w_in.shape[0] == 1, "single-layer block"
    h = x.reshape(batch * seq, d)
    hn = _rmsnorm(h, norm_mix[0], BF16)
    bz = _conv_proj(hn, w_in[0], conv_w[0], seq)
    z = _matmul(hn, w_in[0], F32, col0=COL_Q, n=IN_COLS - COL_Q)
    attn, v_t, u_bf, w_o_bf, w_co_bf, w_ao_bf, w_q_bf = _moba_attention(
        z, v_emb[0], [u_emb[0], w_o[0], w_conv_out[0], w_attn_out[0], w_peer_q[0]], batch, seq)
    merged = _merge(bz, attn, w_co_bf, w_ao_bf, z, b_gate[0])
    h = _matmul_residual(merged, w_o_bf, h)
    hn, hnt = _rmsnorm_with_transpose(h, norm_ffn[0])
    qp = _matmul_bf16(hn, w_q_bf, F32)
    n1, rank2, e1, e2 = _peer_route(qp, sub_keys[0])
    peer_t = _peer_mix(hnt, u_bf, v_t, n1, rank2, e1, e2)
    return _add_rmsnorm(h, peer_t, norm_final).reshape(batch, seq, d)
```
